```python
import math
import jax, jax.numpy as jnp
from jax import lax
import numpy as np

D_MODEL = 1024
BATCH = 8
SEQ = 4096
DEPTH = 1
DEC_BATCH = 32
DEC_SEQ = 1
PAST_LEN = 16384
PAGE_SIZE = 128

HEAD_DIM = 64
MOBA_HEADS = 8
MOBA_W = MOBA_HEADS * HEAD_DIM
DIFF_HEADS = 4
DIFF_VDIM = 2 * HEAD_DIM
DIFF_W = DIFF_HEADS * DIFF_VDIM
MIX_W = MOBA_W + DIFF_W
KV_WIDTH = MIX_W
MOBA_BLOCK = 256
MOBA_TOPK = 3
Q_CHUNK = 128
N_BUCKETS = 32
REL_MAX_DIST = 128
N_BIAS_HEADS = MOBA_HEADS + DIFF_HEADS
N_GROUPS = 4
EXPERTS_PER_GROUP = 8
N_EXPERTS = N_GROUPS * EXPERTS_PER_GROUP
MOE_TOPK = 2
D_EXPERT = D_MODEL // 2
MOE_ROW_BLOCK = 128
D_PLE = 256
ALPHA = (2 * DEPTH) ** 0.25
BETA = (8 * DEPTH) ** -0.25
LN_EPS = 1e-5
ATTN_SCALE = HEAD_DIM ** -0.5
F32 = jnp.float32

kernel_name = 'hymba_moba_diffattn_hmoe_decoder_step'


def layer_norm(x, g, b):
    xf = x.astype(F32)
    xc = xf - jnp.mean(xf, -1, keepdims=True)
    var = jnp.mean(xc * xc, -1, keepdims=True)
    return (xc * lax.rsqrt(var + LN_EPS) * g + b).astype(x.dtype)


def rel_bucket(rel):
    n = jnp.maximum(rel, 0)
    max_exact = N_BUCKETS // 2
    large = max_exact + (jnp.log(jnp.maximum(n, 1).astype(F32) / max_exact)
                         / math.log(REL_MAX_DIST / max_exact) * (N_BUCKETS - max_exact)).astype(jnp.int32)
    large = jnp.minimum(large, N_BUCKETS - 1)
    return jnp.where(n < max_exact, n, large)


def project_qkv(h, w):
    b, s, _ = h.shape
    q, k, v = jnp.split(h @ w, 3, axis=-1)
    qm = q[..., :MOBA_W].reshape(b, s, MOBA_HEADS, HEAD_DIM)
    qd = q[..., MOBA_W:].reshape(b, s, DIFF_HEADS, 2, HEAD_DIM)
    km = k[..., :MOBA_W].reshape(b, s, MOBA_HEADS, HEAD_DIM)
    kd = k[..., MOBA_W:].reshape(b, s, DIFF_HEADS, 2, HEAD_DIM)
    vm = v[..., :MOBA_W].reshape(b, s, MOBA_HEADS, HEAD_DIM)
    vd = v[..., MOBA_W:].reshape(b, s, DIFF_HEADS, DIFF_VDIM)
    return qm, qd, km, kd, vm, vd, k, v


def block_means(k):
    b, t, h, d = k.shape
    return jnp.mean(k.reshape(b, t // MOBA_BLOCK, MOBA_BLOCK, h, d), axis=2, dtype=F32)


def moba_attend(q, q_pos, k, v, k_mean, tab_t):
    nq, nh, d = q.shape
    nb = k.shape[0] // MOBA_BLOCK
    k_blk = k.reshape(nb, MOBA_BLOCK, nh, d)
    v_blk = v.reshape(nb, MOBA_BLOCK, nh, d)
    q_blk = q_pos // MOBA_BLOCK
    gate = jnp.einsum('qhd,nhd->hqn', q, k_mean, preferred_element_type=F32)
    gate = jnp.where(jnp.arange(nb)[None, None, :] < q_blk[None, :, None], gate, -jnp.inf)
    kk = min(MOBA_TOPK, nb)
    _, sel = lax.top_k(gate, kk)
    sel_ok = sel < q_blk[None, :, None]
    blocks = jnp.concatenate([sel, jnp.broadcast_to(q_blk[None, :, None], (nh, nq, 1))], -1)
    ok = jnp.concatenate([sel_ok, jnp.ones((nh, nq, 1), bool)], -1)
    h_idx = jnp.arange(nh)[:, None, None]
    kg = k_blk[blocks, :, h_idx]
    vg = v_blk[blocks, :, h_idx]
    k_pos = blocks[..., None] * MOBA_BLOCK + jnp.arange(MOBA_BLOCK)
    rel = q_pos[None, :, None, None] - k_pos
    s = (jnp.einsum('qhd,hqnbd->hqnb', q, kg, preferred_element_type=F32) * ATTN_SCALE
         + tab_t[h_idx[..., None], rel_bucket(rel)])
    own = (jnp.arange(kk + 1) == kk)[None, None, :, None]
    mask = jnp.where(own, rel >= 0, ok[..., None])
    s = jnp.where(mask, s, -jnp.inf)
    p = jax.nn.softmax(s.reshape(nh, nq, -1), -1).reshape(s.shape)
    out = jnp.einsum('hqnb,hqnbd->qhd', p.astype(v.dtype), vg, preferred_element_type=F32)
    return out.reshape(nq, nh * d).astype(q.dtype)


def moba_prompt(q, k, v, tab_t):
    b, s, nh, d = q.shape
    t_pad = -(-s // MOBA_BLOCK) * MOBA_BLOCK
    padw = ((0, 0), (0, t_pad - s), (0, 0), (0, 0))
    k = jnp.pad(k, padw)
    v = jnp.pad(v, padw)
    k_mean = block_means(k)
    nc = s // Q_CHUNK
    qc = q.reshape(b * nc, Q_CHUNK, nh, d)
    b_idx = jnp.repeat(jnp.arange(b), nc)
    c_idx = jnp.tile(jnp.arange(nc), b)

    def one(args):
        qq, bi, ci = args
        pos = ci * Q_CHUNK + jnp.arange(Q_CHUNK)
        return moba_attend(qq, pos, k[bi], v[bi], k_mean[bi], tab_t)

    return lax.map(one, (qc, b_idx, c_idx)).reshape(b, s, nh * d)


def diff_attend(q, q_pos, segments, lam, lam_init, tab_t, subln_g):
    scores = []
    for k, v, k_pos in segments:
        rel = q_pos[:, None] - k_pos[None, :]
        s = (jnp.einsum('bqhcd,bthcd->bhcqt', q, k, preferred_element_type=F32) * ATTN_SCALE
             + tab_t[:, rel_bucket(rel)][None, :, None])
        scores.append(jnp.where(rel >= 0, s, -jnp.inf))
    p = jax.nn.softmax(jnp.concatenate(scores, -1), -1)
    a = p[:, :, 0] - lam * p[:, :, 1]
    o = None
    off = 0
    for k, v, k_pos in segments:
        t = k_pos.shape[0]
        part = jnp.einsum('bhqt,bthe->bqhe', a[..., off:off + t].astype(v.dtype), v, preferred_element_type=F32)
        o = part if o is None else o + part
        off += t
    o = o * lax.rsqrt(jnp.mean(o * o, -1, keepdims=True) + LN_EPS) * subln_g * (1.0 - lam_init)
    b, nq = q.shape[:2]
    return o.reshape(b, nq, -1).astype(q.dtype)


def diff_prompt(q, k, v, lam, lam_init, tab_t, subln_g):
    b, s = q.shape[:2]
    nc = s // Q_CHUNK
    qc = jnp.moveaxis(q.reshape(b, nc, Q_CHUNK, DIFF_HEADS, 2, HEAD_DIM), 1, 0)
    k_pos = jnp.arange(s)

    def one(args):
        qq, ci = args
        pos = ci * Q_CHUNK + jnp.arange(Q_CHUNK)
        return diff_attend(qq, pos, ((k, v, k_pos),), lam, lam_init, tab_t, subln_g)

    out = lax.map(one, (qc, jnp.arange(nc)))
    return jnp.moveaxis(out, 0, 1).reshape(b, s, DIFF_W)


def attn_prompt(h, w_in_i, lam, lam_init, tab_m, tab_d, subln_g_i):
    qm, qd, km, kd, vm, vd, k_row, v_row = project_qkv(h, w_in_i)
    out_m = moba_prompt(qm, km, vm, tab_m)
    out_d = diff_prompt(qd, kd, vd, lam, lam_init, tab_d, subln_g_i)
    return jnp.concatenate([out_m, out_d], -1), k_row, v_row


def attn_sample(h, cache_k, cache_v, page_table, i, w_in_i, lam, lam_init, tab_m, tab_d, subln_g_i):
    db, ds, _ = h.shape
    past = page_table.shape[1] * PAGE_SIZE
    qm, qd, km, kd, vm, vd, k_row, v_row = project_qkv(h, w_in_i)
    q_pos = past + jnp.arange(ds)
    km_past = cache_k[page_table, i, :, :MOBA_W].reshape(db, past, MOBA_HEADS, HEAD_DIM)
    vm_past = cache_v[page_table, i, :, :MOBA_W].reshape(db, past, MOBA_HEADS, HEAD_DIM)
    kd_past = cache_k[page_table, i, :, MOBA_W:].reshape(db, past, DIFF_HEADS, 2, HEAD_DIM)
    vd_past = cache_v[page_table, i, :, MOBA_W:].reshape(db, past, DIFF_HEADS, DIFF_VDIM)
    t = past + ds
    t_pad = -(-t // MOBA_BLOCK) * MOBA_BLOCK
    zpad = jnp.zeros((db, t_pad - t, MOBA_HEADS, HEAD_DIM), km.dtype)
    km_full = jnp.concatenate([km_past.astype(km.dtype), km, zpad], 1)
    vm_full = jnp.concatenate([vm_past.astype(vm.dtype), vm, zpad], 1)
    k_mean = block_means(km_full)
    out_m = jax.vmap(moba_attend, in_axes=(0, None, 0, 0, 0, None))(qm, q_pos, km_full, vm_full, k_mean, tab_m)
    segments = ((kd_past, vd_past, jnp.arange(past)), (kd, vd, q_pos))
    out_d = diff_attend(qd, q_pos, segments, lam, lam_init, tab_d, subln_g_i)
    return jnp.concatenate([out_m, out_d], -1), k_row, v_row


def hier_moe(x, w_group, b_group, w_er, b_er, w1, w3, w2):
    n, d = x.shape
    g_prob = jax.nn.softmax((x @ w_group + b_group).astype(F32), -1)
    g_gate, g_idx = lax.top_k(g_prob, 1)
    e_all = jnp.einsum('nd,gde->nge', x, w_er) + b_er
    e_logits = jnp.take_along_axis(e_all, g_idx[:, :, None], axis=1)[:, 0].astype(F32)
    top_p, top_i = lax.top_k(jax.nn.softmax(e_logits, -1), MOE_TOPK)
    gates = g_gate * top_p / jnp.sum(top_p, -1, keepdims=True)
    expert = g_idx * EXPERTS_PER_GROUP + top_i
    m = n * MOE_TOPK
    e_flat = expert.reshape(m)
    g_flat = gates.reshape(m)
    tok = jnp.repeat(jnp.arange(n), MOE_TOPK)
    order = jnp.argsort(e_flat)
    e_s, tok_s, g_s = e_flat[order], tok[order], g_flat[order]
    counts = jnp.bincount(e_flat, length=N_EXPERTS)
    padded = (counts + MOE_ROW_BLOCK - 1) // MOE_ROW_BLOCK * MOE_ROW_BLOCK
    start = jnp.cumsum(counts) - counts
    p_end = jnp.cumsum(padded)
    p_start = p_end - padded
    dest = p_start[e_s] + jnp.arange(m) - start[e_s]
    n_blk = -(-(m + N_EXPERTS * (MOE_ROW_BLOCK - 1)) // MOE_ROW_BLOCK)
    rows = n_blk * MOE_ROW_BLOCK
    x_buf = jnp.zeros((rows, d), x.dtype).at[dest].set(x[tok_s])
    blk_e = jnp.minimum(jnp.searchsorted(p_end, jnp.arange(n_blk) * MOE_ROW_BLOCK, side='right'), N_EXPERTS - 1)

    def expert_block(args):
        xb, e = args
        return (jax.nn.silu(xb @ w1[e]) * (xb @ w3[e])) @ w2[e]

    y_buf = lax.map(expert_block, (x_buf.reshape(n_blk, MOE_ROW_BLOCK, d), blk_e)).reshape(rows, d)
    y = jnp.zeros((n, d), F32).at[tok_s].add(y_buf[dest].astype(F32) * g_s[:, None])
    return y.astype(x.dtype)


def finish_layer(h, mix, p_i, w_out_i, ln1_g_i, ln1_b_i, w_group_i, b_group_i, w_er_i, b_er_i,
                 w1_i, w3_i, w2_i, ln2_g_i, ln2_b_i, w_ple_i, w_pg_i, b_pg_i):
    a = layer_norm(ALPHA * h + mix @ w_out_i, ln1_g_i, ln1_b_i)
    moe = hier_moe(a.reshape(-1, a.shape[-1]), w_group_i, b_group_i, w_er_i, b_er_i, w1_i, w3_i, w2_i).reshape(a.shape)
    c = layer_norm(ALPHA * a + moe, ln2_g_i, ln2_b_i)
    return c + jax.nn.sigmoid(c @ w_pg_i + b_pg_i) * (p_i @ w_ple_i)


def setup_inputs(seed: int = 0) -> dict:
    key = jax.random.key(seed)
    ks = jax.random.split(key, 32)
    n_pages = PAST_LEN // PAGE_SIZE
    used = DEC_BATCH * n_pages
    n_pool = used + max(1, used // 4)

    def nrm(k, shape, scale):
        return jax.random.normal(k, shape, F32) * scale

    page_table = jax.random.permutation(ks[4], n_pool)[:used].reshape(DEC_BATCH, n_pages).astype(jnp.int32)
    return {
        'x_prompt': nrm(ks[0], (BATCH, SEQ, D_MODEL), 1.0),
        'x_sample': nrm(ks[1], (DEC_BATCH, DEC_SEQ, D_MODEL), 1.0),
        'cache_k': nrm(ks[2], (n_pool, DEPTH, PAGE_SIZE, KV_WIDTH), 1.0),
        'cache_v': nrm(ks[3], (n_pool, DEPTH, PAGE_SIZE, KV_WIDTH), 1.0),
        'page_table': page_table,
        'p_prompt': nrm(ks[5], (DEPTH, BATCH, SEQ, D_PLE), 1.0),
        'p_sample': nrm(ks[6], (DEPTH, DEC_BATCH, DEC_SEQ, D_PLE), 1.0),
        'rel_bias': nrm(ks[7], (N_BUCKETS, N_BIAS_HEADS), 0.5),
        'w_in': nrm(ks[8], (DEPTH, D_MODEL, 3 * MIX_W), D_MODEL ** -0.5),
        'w_out': nrm(ks[9], (DEPTH, MIX_W, D_MODEL), BETA * MIX_W ** -0.5),
        'lam_q1': nrm(ks[10], (DEPTH, HEAD_DIM), 0.1),
        'lam_k1': nrm(ks[11], (DEPTH, HEAD_DIM), 0.1),
        'lam_q2': nrm(ks[12], (DEPTH, HEAD_DIM), 0.1),
        'lam_k2': nrm(ks[13], (DEPTH, HEAD_DIM), 0.1),
        'subln_g': 1.0 + nrm(ks[14], (DEPTH, DIFF_VDIM), 0.1),
        'ln1_g': 1.0 + nrm(ks[15], (DEPTH, D_MODEL), 0.1),
        'ln1_b': nrm(ks[16], (DEPTH, D_MODEL), 0.1),
        'w_group': nrm(ks[17], (DEPTH, D_MODEL, N_GROUPS), D_MODEL ** -0.5),
        'b_group': nrm(ks[18], (DEPTH, N_GROUPS), 0.01),
        'w_erouter': nrm(ks[19], (DEPTH, N_GROUPS, D_MODEL, EXPERTS_PER_GROUP), D_MODEL ** -0.5),
        'b_erouter': nrm(ks[20], (DEPTH, N_GROUPS, EXPERTS_PER_GROUP), 0.01),
        'w1': nrm(ks[21], (DEPTH, N_EXPERTS, D_MODEL, D_EXPERT), D_MODEL ** -0.5),
        'w3': nrm(ks[22], (DEPTH, N_EXPERTS, D_MODEL, D_EXPERT), D_MODEL ** -0.5),
        'w2': nrm(ks[23], (DEPTH, N_EXPERTS, D_EXPERT, D_MODEL), BETA * D_EXPERT ** -0.5),
        'ln2_g': 1.0 + nrm(ks[24], (DEPTH, D_MODEL), 0.1),
        'ln2_b': nrm(ks[25], (DEPTH, D_MODEL), 0.1),
        'w_ple': nrm(ks[26], (DEPTH, D_PLE, D_MODEL), D_PLE ** -0.5),
        'w_pg': nrm(ks[27], (DEPTH, D_MODEL, D_MODEL), D_MODEL ** -0.5),
        'b_pg': nrm(ks[28], (DEPTH, D_MODEL), 0.1),
    }


def reference(x_prompt, x_sample, cache_k, cache_v, page_table, p_prompt, p_sample, rel_bias,
              w_in, w_out, lam_q1, lam_k1, lam_q2, lam_k2, subln_g, ln1_g, ln1_b,
              w_group, b_group, w_erouter, b_erouter, w1, w3, w2, ln2_g, ln2_b,
              w_ple, w_pg, b_pg):
    tab_m = rel_bias[:, :MOBA_HEADS].T
    tab_d = rel_bias[:, MOBA_HEADS:].T
    hp, hs = x_prompt, x_sample
    kp_rows, vp_rows, ks_rows, vs_rows = [], [], [], []
    for i in range(DEPTH):
        lam_init = 0.8 - 0.6 * math.exp(-0.3 * i)
        lam = (jnp.exp(jnp.sum(lam_q1[i].astype(F32) * lam_k1[i].astype(F32)))
               - jnp.exp(jnp.sum(lam_q2[i].astype(F32) * lam_k2[i].astype(F32))) + lam_init)
        mix_p, k_p, v_p = attn_prompt(hp, w_in[i], lam, lam_init, tab_m, tab_d, subln_g[i])
        mix_s, k_s, v_s = attn_sample(hs, cache_k, cache_v, page_table, i, w_in[i], lam, lam_init,
                                      tab_m, tab_d, subln_g[i])
        layer_w = (w_out[i], ln1_g[i], ln1_b[i], w_group[i], b_group[i], w_erouter[i], b_erouter[i],
                   w1[i], w3[i], w2[i], ln2_g[i], ln2_b[i], w_ple[i], w_pg[i], b_pg[i])
        hp = finish_layer(hp, mix_p, p_prompt[i], *layer_w)
        hs = finish_layer(hs, mix_s, p_sample[i], *layer_w)
        kp_rows.append(k_p)
        vp_rows.append(v_p)
        ks_rows.append(k_s)
        vs_rows.append(v_s)
    y_prompt = hp
    y_sample = hs
    k_new_prompt = jnp.stack(kp_rows, axis=1)
    v_new_prompt = jnp.stack(vp_rows, axis=1)
    k_new_sample = jnp.stack(ks_rows, axis=1)
    v_new_sample = jnp.stack(vs_rows, axis=1)
    return (y_prompt, y_sample, k_new_prompt, v_new_prompt, k_new_sample, v_new_sample)
```

```python
import functools
import math

import jax
import jax.numpy as jnp
from jax import lax
from jax.experimental import pallas as pl
from jax.experimental.pallas import tpu as pltpu

D_MODEL = 1024
PAGE_SIZE = 128
HEAD_DIM = 64
MOBA_HEADS = 8
MOBA_W = MOBA_HEADS * HEAD_DIM
DIFF_HEADS = 4
DIFF_VDIM = 2 * HEAD_DIM
DIFF_W = DIFF_HEADS * DIFF_VDIM
MIX_W = MOBA_W + DIFF_W
MOBA_BLOCK = 256
MOBA_TOPK = 3
N_BUCKETS = 32
REL_MAX_DIST = 128
N_GROUPS = 4
EXPERTS_PER_GROUP = 8
N_EXPERTS = N_GROUPS * EXPERTS_PER_GROUP
D_EXPERT = D_MODEL // 2
D_PLE = 256
LN_EPS = 1e-5
ATTN_SCALE = HEAD_DIM ** -0.5
LANES = 128
F32 = jnp.float32
BF16 = jnp.bfloat16
HIGHEST = lax.Precision.HIGHEST
NEG_INF = float("-inf")
M_INIT = -1e30
VMEM_LIMIT = 48 * 1024 * 1024
NT_DIMS = (((1,), (1,)), ((), ()))


def _cparams(sem):
    return pltpu.CompilerParams(dimension_semantics=sem, vmem_limit_bytes=VMEM_LIMIT)


def _dot(a, b, precise):
    if precise:
        return jnp.dot(a, b, precision=HIGHEST, preferred_element_type=F32)
    return jnp.dot(a.astype(BF16), b.astype(BF16), preferred_element_type=F32)


def _rel_bucket(rel):
    n = jnp.maximum(rel, 0)
    max_exact = N_BUCKETS // 2
    large = max_exact + (jnp.log(jnp.maximum(n, 1).astype(F32) / max_exact)
                         / math.log(REL_MAX_DIST / max_exact) * (N_BUCKETS - max_exact)).astype(jnp.int32)
    large = jnp.minimum(large, N_BUCKETS - 1)
    return jnp.where(n < max_exact, n, large)


def _layer_norm(x, g, b):
    xc = x - jnp.mean(x, -1, keepdims=True)
    var = jnp.mean(xc * xc, -1, keepdims=True)
    return xc * lax.rsqrt(var + LN_EPS) * g + b


def _qkv_prompt_kernel(x_ref, w_ref, k_ref, v_ref, qb_ref, kb_ref, vb_ref, kmean_ref):
    x = x_ref[...].astype(BF16)
    q = jnp.dot(x, w_ref[:, :MIX_W], preferred_element_type=F32)
    qb_ref[...] = q.astype(BF16)
    k = jnp.dot(x, w_ref[:, MIX_W:2 * MIX_W], preferred_element_type=F32)
    k_ref[...] = k
    kb_ref[...] = k.astype(BF16)
    kmean_ref[0] = jnp.sum(k[:, :MOBA_W], axis=0, keepdims=True) * (1.0 / MOBA_BLOCK)
    v = jnp.dot(x, w_ref[:, 2 * MIX_W:], preferred_element_type=F32)
    v_ref[...] = v
    vb_ref[...] = v.astype(BF16)


def _qkv_prompt(x, w_bf16):
    m = x.shape[0]
    tm = MOBA_BLOCK
    row = lambda i: (i, 0)
    return pl.pallas_call(
        _qkv_prompt_kernel,
        grid=(m // tm,),
        in_specs=[pl.BlockSpec((tm, D_MODEL), row),
                  pl.BlockSpec((D_MODEL, 3 * MIX_W), lambda i: (0, 0))],
        out_specs=[pl.BlockSpec((tm, MIX_W), row), pl.BlockSpec((tm, MIX_W), row),
                   pl.BlockSpec((tm, MIX_W), row), pl.BlockSpec((tm, MIX_W), row),
                   pl.BlockSpec((tm, MIX_W), row),
                   pl.BlockSpec((1, 1, MOBA_W), lambda i: (i, 0, 0))],
        out_shape=[jax.ShapeDtypeStruct((m, MIX_W), F32), jax.ShapeDtypeStruct((m, MIX_W), F32),
                   jax.ShapeDtypeStruct((m, MIX_W), BF16), jax.ShapeDtypeStruct((m, MIX_W), BF16),
                   jax.ShapeDtypeStruct((m, MIX_W), BF16),
                   jax.ShapeDtypeStruct((m // tm, 1, MOBA_W), F32)],
        compiler_params=_cparams(("parallel",)),
        name="qkv_prompt",
    )(x, w_bf16)


def _mm_precise_kernel(x_ref, w_ref, o_ref):
    o_ref[...] = jnp.dot(x_ref[...], w_ref[...], precision=HIGHEST, preferred_element_type=F32)


def _mm_precise(x, w, tn):
    m, k = x.shape
    n = w.shape[1]
    return pl.pallas_call(
        _mm_precise_kernel,
        grid=(n // tn,),
        in_specs=[pl.BlockSpec((m, k), lambda j: (0, 0)), pl.BlockSpec((k, tn), lambda j: (0, j))],
        out_specs=pl.BlockSpec((m, tn), lambda j: (0, j)),
        out_shape=jax.ShapeDtypeStruct((m, n), F32),
        compiler_params=_cparams(("parallel",)),
        name="mm_precise",
    )(x, w)


def _online_tile(s, v, m_ref, l_ref, acc_ref):
    m_old = m_ref[...]
    m_new = jnp.maximum(m_old, jnp.max(s, axis=-1, keepdims=True))
    p = jnp.exp(s - m_new)
    alpha = jnp.exp(m_old - m_new)
    l_ref[...] = alpha * l_ref[...] + jnp.sum(p, axis=-1, keepdims=True)
    acc_ref[...] = alpha * acc_ref[...] + jnp.dot(p.astype(BF16), v, preferred_element_type=F32)
    m_ref[...] = m_new


def _causal_stream(qh, k_ref, v_ref, kcols, vcols, i, bias_d, bias_p, sel, m_ref, l_ref, acc_ref):
    t = MOBA_BLOCK
    m_ref[...] = jnp.full(m_ref.shape, M_INIT, F32)
    l_ref[...] = jnp.zeros(l_ref.shape, F32)
    acc_ref[...] = jnp.zeros(acc_ref.shape, F32)
    c_far = bias_p[0:1, 0:1]
    if sel is not None:
        blk_ids = lax.broadcasted_iota(jnp.int32, sel.shape, 1)

    def keep(j, s):
        if sel is None:
            return s
        col = jnp.sum(jnp.where(blk_ids == j, sel, 0.0), axis=-1, keepdims=True)
        return jnp.where(col > 0.5, s, NEG_INF)

    def tile(j, bias):
        rows = pl.ds(pl.multiple_of(j * t, t), t)
        s = lax.dot_general(qh, k_ref[rows, kcols], NT_DIMS, preferred_element_type=F32) + bias
        return s, v_ref[rows, vcols]

    def far_body(j, carry):
        s, v = tile(j, c_far)
        _online_tile(keep(j, s), v, m_ref, l_ref, acc_ref)
        return carry

    lax.fori_loop(0, jnp.maximum(i - 1, 0), far_body, 0)

    @pl.when(i >= 1)
    def _():
        s, v = tile(i - 1, bias_p)
        _online_tile(keep(i - 1, s), v, m_ref, l_ref, acc_ref)

    s, v = tile(i, bias_d)
    r = lax.broadcasted_iota(jnp.int32, (t, t), 0)
    c = lax.broadcasted_iota(jnp.int32, (t, t), 1)
    _online_tile(jnp.where(r >= c, s, NEG_INF), v, m_ref, l_ref, acc_ref)


def _top_blocks(gate, n_valid):
    nb = gate.shape[-1]
    ids = lax.broadcasted_iota(jnp.int32, gate.shape, 1).astype(F32)
    g = jnp.where(ids < n_valid.astype(F32), gate, NEG_INF)
    sel = jnp.zeros(gate.shape, F32)
    for _ in range(MOBA_TOPK):
        mx = jnp.max(g, axis=-1, keepdims=True)
        cand = jnp.where(g == mx, jnp.where(mx > NEG_INF, ids, float(nb)), float(nb))
        pick = ids == jnp.min(cand, axis=-1, keepdims=True)
        sel = jnp.where(pick, 1.0, sel)
        g = jnp.where(pick, NEG_INF, g)
    return sel


def _moba_prompt_kernel(q_ref, k_ref, v_ref, kmean_ref, bd_ref, bp_ref, o_ref, m_ref, l_ref, acc_ref):
    i = pl.program_id(1)
    for h in range(MOBA_HEADS):
        cols = slice(h * HEAD_DIM, (h + 1) * HEAD_DIM)
        qh = q_ref[:, cols]
        gate = lax.dot_general(qh, kmean_ref[:, cols].astype(BF16), NT_DIMS, preferred_element_type=F32)
        sel = _top_blocks(gate, i)
        _causal_stream(qh * ATTN_SCALE, k_ref, v_ref, cols, cols, i, bd_ref[h], bp_ref[h], sel,
                       m_ref, l_ref, acc_ref)
        o_ref[:, cols] = (acc_ref[...] / l_ref[...]).astype(o_ref.dtype)


def _lambda(lam_ref, lam_init):
    lv = lam_ref[...]
    s1 = jnp.sum(lv[0:1] * lv[1:2], axis=-1, keepdims=True)
    s2 = jnp.sum(lv[2:3] * lv[3:4], axis=-1, keepdims=True)
    return jnp.exp(s1) - jnp.exp(s2) + lam_init


def _sub_norm(o, g, lam_init):
    return o * lax.rsqrt(jnp.mean(o * o, -1, keepdims=True) + LN_EPS) * g * (1.0 - lam_init)


def _diff_prompt_kernel(q_ref, k_ref, v_ref, bd_ref, bp_ref, lam_ref, g_ref, o_ref,
                        m_ref, l_ref, acc_ref, *, lam_init):
    i = pl.program_id(1)
    lam = _lambda(lam_ref, lam_init)
    for h in range(DIFF_HEADS):
        vcols = slice(h * DIFF_VDIM, (h + 1) * DIFF_VDIM)
        parts = []
        for c in range(2):
            cols = slice(h * DIFF_VDIM + c * HEAD_DIM, h * DIFF_VDIM + (c + 1) * HEAD_DIM)
            _causal_stream(q_ref[:, cols] * ATTN_SCALE, k_ref, v_ref, cols, vcols, i, bd_ref[h], bp_ref[h],
                           None, m_ref, l_ref, acc_ref)
            parts.append(acc_ref[...] / l_ref[...])
        o = parts[0] - lam * parts[1]
        o_ref[:, vcols] = _sub_norm(o, g_ref[...], lam_init).astype(o_ref.dtype)


def _prompt_bias_tiles(tab):
    r = jnp.arange(MOBA_BLOCK)[:, None]
    c = jnp.arange(MOBA_BLOCK)[None, :]
    return tab[:, _rel_bucket(r - c)], tab[:, _rel_bucket(r - c + MOBA_BLOCK)]


def _moba_prompt(qb, kb, vb, kmean, tab_m, b, s):
    t = MOBA_BLOCK
    nq = s // t
    bd, bp = _prompt_bias_tiles(tab_m)
    const3 = lambda bi, i: (0, 0, 0)
    return pl.pallas_call(
        _moba_prompt_kernel,
        grid=(b, nq),
        in_specs=[pl.BlockSpec((t, MOBA_W), lambda bi, i: (bi * nq + i, 0)),
                  pl.BlockSpec((s, MOBA_W), lambda bi, i: (bi, 0)),
                  pl.BlockSpec((s, MOBA_W), lambda bi, i: (bi, 0)),
                  pl.BlockSpec((None, nq, MOBA_W), lambda bi, i: (bi, 0, 0)),
                  pl.BlockSpec((MOBA_HEADS, t, t), const3),
                  pl.BlockSpec((MOBA_HEADS, t, t), const3)],
        out_specs=pl.BlockSpec((t, MOBA_W), lambda bi, i: (bi * nq + i, 0)),
        out_shape=jax.ShapeDtypeStruct((b * s, MOBA_W), BF16),
        scratch_shapes=[pltpu.VMEM((t, 1), F32), pltpu.VMEM((t, 1), F32), pltpu.VMEM((t, HEAD_DIM), F32)],
        compiler_params=_cparams(("parallel", "arbitrary")),
        name="moba_prompt",
    )(qb, kb, vb, kmean.reshape(b, nq, MOBA_W), bd, bp)


def _diff_prompt(qb, kb, vb, tab_d, lam_vecs, subln_g, lam_init, b, s):
    t = MOBA_BLOCK
    nq = s // t
    bd, bp = _prompt_bias_tiles(tab_d)
    const3 = lambda bi, i: (0, 0, 0)
    const2 = lambda bi, i: (0, 0)
    return pl.pallas_call(
        functools.partial(_diff_prompt_kernel, lam_init=lam_init),
        grid=(b, nq),
        in_specs=[pl.BlockSpec((t, DIFF_W), lambda bi, i: (bi * nq + i, 1)),
                  pl.BlockSpec((s, DIFF_W), lambda bi, i: (bi, 1)),
                  pl.BlockSpec((s, DIFF_W), lambda bi, i: (bi, 1)),
                  pl.BlockSpec((DIFF_HEADS, t, t), const3),
                  pl.BlockSpec((DIFF_HEADS, t, t), const3),
                  pl.BlockSpec((4, HEAD_DIM), const2),
                  pl.BlockSpec((1, DIFF_VDIM), const2)],
        out_specs=pl.BlockSpec((t, DIFF_W), lambda bi, i: (bi * nq + i, 0)),
        out_shape=jax.ShapeDtypeStruct((b * s, DIFF_W), BF16),
        scratch_shapes=[pltpu.VMEM((t, 1), F32), pltpu.VMEM((t, 1), F32), pltpu.VMEM((t, DIFF_VDIM), F32)],
        compiler_params=_cparams(("parallel", "arbitrary")),
        name="diff_prompt",
    )(qb, kb, vb, bd, bp, lam_vecs, subln_g)


def _sample_main_kernel(pt_ref, q_ref, knew_ref, vnew_ref, ka_ref, kb_ref, va_ref, vb_ref,
                        bias_ref, bias0_ref, lam_ref, g_ref,
                        o_ref, sel_ref,
                        qmat_ref, gacc_ref, m_ref, l_ref, acc_ref, *, lam_init):
    del pt_ref
    j = pl.program_id(1)
    nblk = pl.num_programs(1)
    n_sub = 2 * DIFF_HEADS
    sub_row = lax.broadcasted_iota(jnp.int32, (n_sub, DIFF_W), 0)
    sub_of_lane = lax.broadcasted_iota(jnp.int32, (n_sub, DIFF_W), 1) // HEAD_DIM

    @pl.when(j == 0)
    def _():
        qd = jnp.broadcast_to(q_ref[:, MOBA_W:], (n_sub, DIFF_W))
        qmat_ref[...] = jnp.where(sub_row == sub_of_lane, qd, 0.0) * ATTN_SCALE
        m_ref[...] = jnp.full(m_ref.shape, M_INIT, F32)
        l_ref[...] = jnp.zeros(l_ref.shape, F32)
        acc_ref[...] = jnp.zeros(acc_ref.shape, F32)

    def update(s, pv):
        m_old = m_ref[...]
        m_new = jnp.maximum(m_old, jnp.max(s, axis=-1, keepdims=True))
        p = jnp.exp(s - m_new)
        alpha = jnp.exp(m_old - m_new)
        l_ref[...] = alpha * l_ref[...] + jnp.sum(p, axis=-1, keepdims=True)
        acc_ref[...] = alpha * acc_ref[...] + pv(p)
        m_ref[...] = m_new

    ksum = jnp.zeros((1, MOBA_W), F32)
    for half, (k_ref, v_ref) in enumerate(((ka_ref, va_ref), (kb_ref, vb_ref))):
        ksum = ksum + jnp.sum(k_ref[:, :MOBA_W], axis=0, keepdims=True)
        s = lax.dot_general(qmat_ref[...], k_ref[:, MOBA_W:], NT_DIMS, precision=HIGHEST,
                            preferred_element_type=F32)
        s = s + bias_ref[:, half * PAGE_SIZE:(half + 1) * PAGE_SIZE]
        v = v_ref[...]
        update(s, lambda p, v=v: jnp.dot(p, v, precision=HIGHEST, preferred_element_type=F32))
    gacc_ref[pl.ds(j, 1), :] = ksum * (1.0 / MOBA_BLOCK) * q_ref[:, :MOBA_W]

    @pl.when(j == nblk - 1)
    def _():
        s_self = jnp.sum(qmat_ref[...] * knew_ref[:, MOBA_W:], axis=-1, keepdims=True) + bias0_ref[:, 0:1]
        update(s_self, lambda p: p * vnew_ref[:, MOBA_W:])
        o = acc_ref[...] / l_ref[...]
        lam = _lambda(lam_ref, lam_init)
        for h in range(DIFF_HEADS):
            vcols = slice(h * DIFF_VDIM, (h + 1) * DIFF_VDIM)
            oh = o[2 * h:2 * h + 1, vcols] - lam * o[2 * h + 1:2 * h + 2, vcols]
            o_ref[:, vcols] = _sub_norm(oh, g_ref[...], lam_init)
        head_row = lax.broadcasted_iota(jnp.int32, (MOBA_HEADS, MOBA_W), 0)
        head_of_lane = lax.broadcasted_iota(jnp.int32, (MOBA_HEADS, MOBA_W), 1) // HEAD_DIM
        seg = jnp.where(head_row == head_of_lane, 1.0, 0.0)
        gate = lax.dot_general(seg, gacc_ref[...], NT_DIMS, precision=HIGHEST, preferred_element_type=F32)
        ids = lax.broadcasted_iota(jnp.int32, gate.shape, 1).astype(F32)
        out_lane = lax.broadcasted_iota(jnp.int32, (MOBA_HEADS, LANES), 1)
        picked = jnp.zeros((MOBA_HEADS, LANES), F32)
        for t in range(MOBA_TOPK):
            mx = jnp.max(gate, axis=-1, keepdims=True)
            idx = jnp.min(jnp.where(gate == mx, ids, float(gate.shape[-1])), axis=-1, keepdims=True)
            picked = jnp.where(out_lane == t, idx, picked)
            gate = jnp.where(ids == idx, NEG_INF, gate)
        sel_ref[...] = picked.astype(jnp.int32)


def _sample_main(page_table, q_s, k_s, v_s, cache_k, cache_v, layer, tab_d, lam_vecs, subln_g, lam_init):
    db, n_pages = page_table.shape
    past = n_pages * PAGE_SIZE
    assert past % MOBA_BLOCK == 0 and past // MOBA_BLOCK >= MOBA_TOPK
    nblk = past // MOBA_BLOCK
    ppb = MOBA_BLOCK // PAGE_SIZE
    n_sub = 2 * DIFF_HEADS
    tab_sub = jnp.repeat(tab_d, 2, axis=0)
    bias_past = tab_sub[:, _rel_bucket(past - jnp.arange(past))]
    bias_self = jnp.broadcast_to(tab_sub[:, 0:1], (n_sub, LANES))

    def page(half):
        return lambda b, j, pt: (pt[b * n_pages + ppb * j + half], layer, 0, 0)

    def vpage(half):
        return lambda b, j, pt: (pt[b * n_pages + ppb * j + half], layer, 0, 1)

    tok = lambda b, j, pt: (b, 0, 0)
    const2 = lambda b, j, pt: (0, 0)
    grid_spec = pltpu.PrefetchScalarGridSpec(
        num_scalar_prefetch=1,
        grid=(db, nblk),
        in_specs=[pl.BlockSpec((None, 1, MIX_W), tok), pl.BlockSpec((None, 1, MIX_W), tok),
                  pl.BlockSpec((None, 1, MIX_W), tok),
                  pl.BlockSpec((None, None, PAGE_SIZE, MIX_W), page(0)),
                  pl.BlockSpec((None, None, PAGE_SIZE, MIX_W), page(1)),
                  pl.BlockSpec((None, None, PAGE_SIZE, DIFF_W), vpage(0)),
                  pl.BlockSpec((None, None, PAGE_SIZE, DIFF_W), vpage(1)),
                  pl.BlockSpec((n_sub, MOBA_BLOCK), lambda b, j, pt: (0, j)),
                  pl.BlockSpec((n_sub, LANES), const2),
                  pl.BlockSpec((4, HEAD_DIM), const2),
                  pl.BlockSpec((1, DIFF_VDIM), const2)],
        out_specs=[pl.BlockSpec((None, 1, DIFF_W), tok),
                   pl.BlockSpec((None, MOBA_HEADS, LANES), tok)],
        scratch_shapes=[pltpu.VMEM((n_sub, DIFF_W), F32), pltpu.VMEM((nblk, MOBA_W), F32),
                        pltpu.VMEM((n_sub, 1), F32), pltpu.VMEM((n_sub, 1), F32),
                        pltpu.VMEM((n_sub, DIFF_W), F32)],
    )
    o_d, sel = pl.pallas_call(
        functools.partial(_sample_main_kernel, lam_init=lam_init),
        grid_spec=grid_spec,
        out_shape=[jax.ShapeDtypeStruct((db, 1, DIFF_W), F32),
                   jax.ShapeDtypeStruct((db, MOBA_HEADS, LANES), jnp.int32)],
        compiler_params=_cparams(("parallel", "arbitrary")),
        name="sample_main",
    )(page_table.reshape(-1), q_s, k_s, v_s, cache_k, cache_k, cache_v, cache_v,
      bias_past, bias_self, lam_vecs, subln_g)
    return o_d.reshape(db, DIFF_W), sel[:, :, :MOBA_TOPK]


def _sample_moba_kernel(pt_ref, sel_ref, q_ref, knew_ref, vnew_ref, ka_ref, kb_ref, va_ref, vb_ref,
                        bias_ref, bias0_ref, o_ref, m_ref, l_ref, acc_ref):
    del pt_ref, sel_ref
    h = pl.program_id(1)
    t = pl.program_id(2)
    rows = m_ref.shape[0]
    par = h % 2
    lane = lax.broadcasted_iota(jnp.int32, (rows, LANES), 1)
    mine = (lane >= par * HEAD_DIM) & (lane < (par + 1) * HEAD_DIM)
    qm = jnp.where(mine, jnp.broadcast_to(q_ref[...], (rows, LANES)), 0.0) * ATTN_SCALE

    @pl.when(t == 0)
    def _():
        m_ref[...] = jnp.full(m_ref.shape, M_INIT, F32)
        l_ref[...] = jnp.zeros(l_ref.shape, F32)
        acc_ref[...] = jnp.zeros(acc_ref.shape, F32)

    def update(s, pv):
        m_old = m_ref[...]
        m_new = jnp.maximum(m_old, jnp.max(s, axis=-1, keepdims=True))
        p = jnp.exp(s - m_new)
        alpha = jnp.exp(m_old - m_new)
        l_ref[...] = alpha * l_ref[...] + jnp.sum(p, axis=-1, keepdims=True)
        acc_ref[...] = alpha * acc_ref[...] + pv(p)
        m_ref[...] = m_new

    for half, (k_ref, v_ref) in enumerate(((ka_ref, va_ref), (kb_ref, vb_ref))):
        s = lax.dot_general(qm, k_ref[...], NT_DIMS, precision=HIGHEST, preferred_element_type=F32)
        s = s + bias_ref[:, half * PAGE_SIZE:(half + 1) * PAGE_SIZE]
        v = v_ref[...]
        update(s, lambda p, v=v: jnp.dot(p, v, precision=HIGHEST, preferred_element_type=F32))

    @pl.when(t == pl.num_programs(2) - 1)
    def _():
        s_self = jnp.sum(qm * knew_ref[...], axis=-1, keepdims=True) + bias0_ref[:, 0:1]
        update(s_self, lambda p: p * vnew_ref[...])
        o = acc_ref[...] / l_ref[...]
        o_ref[...] = jnp.where(par == 0, o[0:1, :HEAD_DIM], o[0:1, HEAD_DIM:])


def _sample_moba(page_table, sel, q_s, k_s, v_s, cache_k, cache_v, layer, tab_m):
    db, n_pages = page_table.shape
    past = n_pages * PAGE_SIZE
    ppb = MOBA_BLOCK // PAGE_SIZE
    rows = 8
    bias_past = tab_m[:, _rel_bucket(past - jnp.arange(past))].reshape(MOBA_HEADS, 1, past)
    bias_self = jnp.broadcast_to(tab_m[:, 0:1, None], (MOBA_HEADS, 1, LANES))

    def sel_at(b, h, t, sl):
        return sl[(b * MOBA_HEADS + h) * MOBA_TOPK + t]

    def page(half):
        return lambda b, h, t, pt, sl: (pt[b * n_pages + ppb * sel_at(b, h, t, sl) + half], layer, 0, h // 2)

    tok = lambda b, h, t, pt, sl: (b, 0, h // 2)
    grid_spec = pltpu.PrefetchScalarGridSpec(
        num_scalar_prefetch=2,
        grid=(db, MOBA_HEADS, MOBA_TOPK),
        in_specs=[pl.BlockSpec((None, 1, LANES), tok), pl.BlockSpec((None, 1, LANES), tok),
                  pl.BlockSpec((None, 1, LANES), tok),
                  pl.BlockSpec((None, None, PAGE_SIZE, LANES), page(0)),
                  pl.BlockSpec((None, None, PAGE_SIZE, LANES), page(1)),
                  pl.BlockSpec((None, None, PAGE_SIZE, LANES), page(0)),
                  pl.BlockSpec((None, None, PAGE_SIZE, LANES), page(1)),
                  pl.BlockSpec((None, 1, MOBA_BLOCK), lambda b, h, t, pt, sl: (h, 0, sel_at(b, h, t, sl))),
                  pl.BlockSpec((None, 1, LANES), lambda b, h, t, pt, sl: (h, 0, 0))],
        out_specs=pl.BlockSpec((None, None, 1, HEAD_DIM), lambda b, h, t, pt, sl: (b, h, 0, 0)),
        scratch_shapes=[pltpu.VMEM((rows, 1), F32), pltpu.VMEM((rows, 1), F32), pltpu.VMEM((rows, LANES), F32)],
    )
    o_m = pl.pallas_call(
        _sample_moba_kernel,
        grid_spec=grid_spec,
        out_shape=jax.ShapeDtypeStruct((db, MOBA_HEADS, 1, HEAD_DIM), F32),
        compiler_params=_cparams(("parallel", "arbitrary", "arbitrary")),
        name="sample_moba",
    )(page_table.reshape(-1), sel.reshape(-1), q_s, k_s, v_s, cache_k, cache_k, cache_v, cache_v,
      bias_past, bias_self)
    return o_m.reshape(db, MOBA_W)


def _post_attn_kernel(h_ref, mm_ref, md_ref, wo_ref, g_ref, b_ref, wr_ref, br_ref,
                      a_ref, eid_ref, gate_ref, *, alpha, precise):
    mix = _dot(mm_ref[...], wo_ref[:MOBA_W, :], precise) + _dot(md_ref[...], wo_ref[MOBA_W:, :], precise)
    a = _layer_norm(alpha * h_ref[...] + mix, g_ref[...], b_ref[...])
    a_ref[...] = a
    logits = jnp.dot(a, wr_ref[...], precision=HIGHEST, preferred_element_type=F32) + br_ref[...]
    lane = lax.broadcasted_iota(jnp.int32, logits.shape, 1).astype(F32)
    none = float(LANES)
    gl = jnp.where(lane < N_GROUPS, logits, NEG_INF)
    gmax = jnp.max(gl, axis=-1, keepdims=True)
    g_gate = 1.0 / jnp.sum(jnp.exp(gl - gmax), axis=-1, keepdims=True)
    g_idx = jnp.min(jnp.where(gl == gmax, lane, none), axis=-1, keepdims=True)
    lo = N_GROUPS + EXPERTS_PER_GROUP * g_idx
    el = jnp.where(lane >= lo, jnp.where(lane < lo + EXPERTS_PER_GROUP, logits, NEG_INF), NEG_INF)
    m1 = jnp.max(el, axis=-1, keepdims=True)
    i1 = jnp.min(jnp.where(el == m1, lane, none), axis=-1, keepdims=True)
    el2 = jnp.where(lane == i1, NEG_INF, el)
    m2 = jnp.max(el2, axis=-1, keepdims=True)
    i2 = jnp.min(jnp.where(el2 == m2, lane, none), axis=-1, keepdims=True)
    r = jnp.exp(m2 - m1)
    w1 = g_gate / (1.0 + r)
    w2 = g_gate * r / (1.0 + r)
    eid_ref[...] = jnp.where(lane == 0.0, i1 - N_GROUPS, jnp.where(lane == 1.0, i2 - N_GROUPS, 0.0)).astype(jnp.int32)
    gate_ref[...] = jnp.where(lane == 0.0, w1, jnp.where(lane == 1.0, w2, 0.0))


def _post_attn(h, mix_m, mix_d, w_out, ln_g, ln_b, w_router, b_router, alpha, precise, tm):
    m = h.shape[0]
    row = lambda i: (i, 0)
    const = lambda i: (0, 0)
    return pl.pallas_call(
        functools.partial(_post_attn_kernel, alpha=alpha, precise=precise),
        grid=(m // tm,),
        in_specs=[pl.BlockSpec((tm, D_MODEL), row), pl.BlockSpec((tm, MOBA_W), row),
                  pl.BlockSpec((tm, DIFF_W), row), pl.BlockSpec((MIX_W, D_MODEL), const),
                  pl.BlockSpec((1, D_MODEL), const), pl.BlockSpec((1, D_MODEL), const),
                  pl.BlockSpec((D_MODEL, LANES), const), pl.BlockSpec((1, LANES), const)],
        out_specs=[pl.BlockSpec((tm, D_MODEL), row), pl.BlockSpec((tm, LANES), row),
                   pl.BlockSpec((tm, LANES), row)],
        out_shape=[jax.ShapeDtypeStruct((m, D_MODEL), F32), jax.ShapeDtypeStruct((m, LANES), jnp.int32),
                   jax.ShapeDtypeStruct((m, LANES), F32)],
        compiler_params=_cparams(("parallel",)),
        name="post_attn",
    )(h, mix_m, mix_d, w_out, ln_g, ln_b, w_router, b_router)


def _moe_kernel(meta_ref, blk_e_ref, nblk_ref, a_hbm, w1_ref, w3_ref, w2_ref, y_hbm,
                xbuf, ybuf, w1b, w3b, w2b, gsem, ssem, *, rb, n_tok):
    i = pl.program_id(0)
    n_used = nblk_ref[0]

    def gather_copy(blk, r, slot):
        tok = meta_ref[blk * rb + r] >> 2
        return pltpu.make_async_copy(a_hbm.at[pl.ds(tok, 1)], xbuf.at[slot, pl.ds(r, 1)], gsem.at[slot])

    def scatter_copy(blk, r, slot):
        code = meta_ref[blk * rb + r]
        dst = (code >> 2) + ((code >> 1) & 1) * n_tok
        return pltpu.make_async_copy(ybuf.at[slot, pl.ds(r, 1)], y_hbm.at[pl.ds(dst, 1)], ssem.at[slot])

    def for_rows(fn):
        def body(r, carry):
            fn(r)
            return carry
        lax.fori_loop(0, rb, body, 0)

    def for_valid_rows(blk, fn):
        def go(r):
            @pl.when((meta_ref[blk * rb + r] & 1) == 1)
            def _():
                fn(r)
        for_rows(go)

    @pl.when(i < n_used)
    def _():
        slot = i % 2

        @pl.when(i == 0)
        def _():
            for_rows(lambda r: gather_copy(0, r, 0).start())

        @pl.when(i + 1 < n_used)
        def _():
            for_rows(lambda r: gather_copy(i + 1, r, 1 - slot).start())

        e = blk_e_ref[i]

        @pl.when((i == 0) | (e != blk_e_ref[jnp.maximum(i - 1, 0)]))
        def _():
            w1b[...] = w1_ref[...].astype(BF16)
            w3b[...] = w3_ref[...].astype(BF16)
            w2b[...] = w2_ref[...].astype(BF16)

        for_rows(lambda r: gather_copy(i, r, slot).wait())
        x = xbuf[slot].astype(BF16)
        h1 = jnp.dot(x, w1b[...], preferred_element_type=F32)
        h3 = jnp.dot(x, w3b[...], preferred_element_type=F32)
        hidden = (h1 * jax.nn.sigmoid(h1) * h3).astype(BF16)
        y = jnp.dot(hidden, w2b[...], preferred_element_type=F32)

        @pl.when(i >= 2)
        def _():
            for_valid_rows(i - 2, lambda r: scatter_copy(i - 2, r, slot).wait())

        ybuf[slot] = y
        for_valid_rows(i, lambda r: scatter_copy(i, r, slot).start())

        @pl.when(i == n_used - 1)
        def _():
            @pl.when(i >= 1)
            def _():
                for_valid_rows(i - 1, lambda r: scatter_copy(i - 1, r, 1 - slot).wait())
            for_valid_rows(i, lambda r: scatter_copy(i, r, slot).wait())


def _route_meta(eid, rb):
    n = eid.shape[0]
    m = n * 2
    e_flat = eid.reshape(m)
    order = jnp.argsort(e_flat).astype(jnp.int32)
    e_s = e_flat[order]
    counts = jnp.bincount(e_flat, length=N_EXPERTS).astype(jnp.int32)
    padded = (counts + rb - 1) // rb * rb
    start = jnp.cumsum(counts) - counts
    p_end = jnp.cumsum(padded)
    p_start = p_end - padded
    dest = p_start[e_s] + jnp.arange(m, dtype=jnp.int32) - start[e_s]
    n_blk = -(-(m + N_EXPERTS * (rb - 1)) // rb)
    meta = jnp.zeros((n_blk * rb,), jnp.int32).at[dest].set(order * 2 + 1)
    blk_e = jnp.minimum(jnp.searchsorted(p_end, jnp.arange(n_blk, dtype=jnp.int32) * rb, side='right'),
                        N_EXPERTS - 1).astype(jnp.int32)
    n_used = (p_end[-1:] // rb).astype(jnp.int32)
    return meta, blk_e, n_used, n_blk


def _moe(a, eid, w1, w3, w2, rb):
    n = a.shape[0]
    meta, blk_e, n_used, n_blk = _route_meta(eid, rb)
    wmap = lambda i, meta, be, nu: (be[i], 0, 0)
    grid_spec = pltpu.PrefetchScalarGridSpec(
        num_scalar_prefetch=3,
        grid=(n_blk,),
        in_specs=[pl.BlockSpec(memory_space=pl.ANY),
                  pl.BlockSpec((None, D_MODEL, D_EXPERT), wmap),
                  pl.BlockSpec((None, D_MODEL, D_EXPERT), wmap),
                  pl.BlockSpec((None, D_EXPERT, D_MODEL), wmap)],
        out_specs=pl.BlockSpec(memory_space=pl.ANY),
        scratch_shapes=[pltpu.VMEM((2, rb, D_MODEL), F32), pltpu.VMEM((2, rb, D_MODEL), F32),
                        pltpu.VMEM((D_MODEL, D_EXPERT), BF16), pltpu.VMEM((D_MODEL, D_EXPERT), BF16),
                        pltpu.VMEM((D_EXPERT, D_MODEL), BF16),
                        pltpu.SemaphoreType.DMA((2,)), pltpu.SemaphoreType.DMA((2,))],
    )
    y = pl.pallas_call(
        functools.partial(_moe_kernel, rb=rb, n_tok=n),
        grid_spec=grid_spec,
        out_shape=jax.ShapeDtypeStruct((2 * n, D_MODEL), F32),
        compiler_params=_cparams(("arbitrary",)),
        name="moe",
    )(meta, blk_e, n_used, a, w1, w3, w2)
    return y.reshape(2, n, D_MODEL)


def _finish_kernel(a_ref, y_ref, gate_ref, p_ref, g_ref, b_ref, wpg_ref, bpg_ref, wple_ref, o_ref,
                   *, alpha, precise):
    gates = gate_ref[...]
    moe = y_ref[0] * gates[:, 0:1] + y_ref[1] * gates[:, 1:2]
    c = _layer_norm(alpha * a_ref[...] + moe, g_ref[...], b_ref[...])
    gate = jax.nn.sigmoid(_dot(c, wpg_ref[...], precise) + bpg_ref[...])
    o_ref[...] = c + gate * _dot(p_ref[...], wple_ref[...], precise)


def _finish(a, y, gates, p, ln_g, ln_b, w_pg, b_pg, w_ple, alpha, precise, tm):
    m = a.shape[0]
    row = lambda i: (i, 0)
    const = lambda i: (0, 0)
    return pl.pallas_call(
        functools.partial(_finish_kernel, alpha=alpha, precise=precise),
        grid=(m // tm,),
        in_specs=[pl.BlockSpec((tm, D_MODEL), row), pl.BlockSpec((2, tm, D_MODEL), lambda i: (0, i, 0)),
                  pl.BlockSpec((tm, LANES), row), pl.BlockSpec((tm, D_PLE), row),
                  pl.BlockSpec((1, D_MODEL), const), pl.BlockSpec((1, D_MODEL), const),
                  pl.BlockSpec((D_MODEL, D_MODEL), const), pl.BlockSpec((1, D_MODEL), const),
                  pl.BlockSpec((D_PLE, D_MODEL), const)],
        out_specs=pl.BlockSpec((tm, D_MODEL), row),
        out_shape=jax.ShapeDtypeStruct((m, D_MODEL), F32),
        compiler_params=_cparams(("parallel",)),
        name="finish",
    )(a, y, gates, p, ln_g, ln_b, w_pg, b_pg, w_ple)


def _pack_router(w_group, b_group, w_er, b_er):
    w = jnp.concatenate([w_group, jnp.transpose(w_er, (1, 0, 2)).reshape(D_MODEL, N_EXPERTS)], axis=1)
    b = jnp.concatenate([b_group, b_er.reshape(N_EXPERTS)])
    pad = LANES - w.shape[1]
    return jnp.pad(w, ((0, 0), (0, pad))), jnp.pad(b, (0, pad)).reshape(1, LANES)


def kernel(x_prompt, x_sample, cache_k, cache_v, page_table, p_prompt, p_sample, rel_bias, w_in, w_out, lam_q1, lam_k1, lam_q2, lam_k2, subln_g, ln1_g, ln1_b, w_group, b_group, w_erouter, b_erouter, w1, w3, w2, ln2_g, ln2_b, w_ple, w_pg, b_pg):
    depth = w_in.shape[0]
    b, s, _ = x_prompt.shape
    db, ds, _ = x_sample.shape
    assert ds == 1 and s % MOBA_BLOCK == 0
    alpha = (2 * depth) ** 0.25
    tab_m = rel_bias[:, :MOBA_HEADS].T
    tab_d = rel_bias[:, MOBA_HEADS:].T
    hp = x_prompt.reshape(b * s, D_MODEL)
    hs = x_sample.reshape(db, D_MODEL)
    kp_rows, vp_rows, ks_rows, vs_rows = [], [], [], []
    for i in range(depth):
        lam_init = 0.8 - 0.6 * math.exp(-0.3 * i)
        lam_vecs = jnp.stack([lam_q1[i], lam_k1[i], lam_q2[i], lam_k2[i]]).astype(F32)
        g_sub = subln_g[i].reshape(1, DIFF_VDIM)
        w_router, b_router = _pack_router(w_group[i], b_group[i], w_erouter[i], b_erouter[i])
        ln1 = (ln1_g[i].reshape(1, D_MODEL), ln1_b[i].reshape(1, D_MODEL))
        ln2 = (ln2_g[i].reshape(1, D_MODEL), ln2_b[i].reshape(1, D_MODEL))
        b_pg_i = b_pg[i].reshape(1, D_MODEL)

        k_p, v_p, qb, kb, vb, kmean = _qkv_prompt(hp, w_in[i].astype(BF16))
        mix_m = _moba_prompt(qb, kb, vb, kmean, tab_m, b, s)
        mix_d = _diff_prompt(qb, kb, vb, tab_d, lam_vecs, g_sub, lam_init, b, s)
        a_p, eid_p, gate_p = _post_attn(hp, mix_m, mix_d, w_out[i].astype(BF16), *ln1, w_router, b_router,
                                        alpha, False, 256)
        y_p = _moe(a_p, eid_p[:, :2], w1[i], w3[i], w2[i], 256)
        hp = _finish(a_p, y_p, gate_p, p_prompt[i].reshape(b * s, D_PLE), *ln2, w_pg[i].astype(BF16), b_pg_i,
                     w_ple[i].astype(BF16), alpha, False, 256)

        qkv_s = _mm_precise(hs, w_in[i], MIX_W)
        q_s = qkv_s[:, :MIX_W].reshape(db, 1, MIX_W)
        k_s = qkv_s[:, MIX_W:2 * MIX_W].reshape(db, 1, MIX_W)
        v_s = qkv_s[:, 2 * MIX_W:].reshape(db, 1, MIX_W)
        o_d, sel = _sample_main(page_table, q_s, k_s, v_s, cache_k, cache_v, i, tab_d, lam_vecs, g_sub, lam_init)
        o_m = _sample_moba(page_table, sel, q_s, k_s, v_s, cache_k, cache_v, i, tab_m)
        a_s, eid_s, gate_s = _post_attn(hs, o_m, o_d, w_out[i], *ln1, w_router, b_router, alpha, True, db)
        y_s = _moe(a_s, eid_s[:, :2], w1[i], w3[i], w2[i], 64)
        hs = _finish(a_s, y_s, gate_s, p_sample[i].reshape(db, D_PLE), *ln2, w_pg[i], b_pg_i, w_ple[i],
                     alpha, True, db)

        kp_rows.append(k_p.reshape(b, s, MIX_W))
        vp_rows.append(v_p.reshape(b, s, MIX_W))
        ks_rows.append(k_s.reshape(db, ds, MIX_W))
        vs_rows.append(v_s.reshape(db, ds, MIX_W))
    return (hp.reshape(b, s, D_MODEL), hs.reshape(db, ds, D_MODEL),
            jnp.stack(kp_rows, axis=1), jnp.stack(vp_rows, axis=1),
            jnp.stack(ks_rows, axis=1), jnp.stack(vs_rows, axis=1))
```

```python
import functools
import math

import jax
import jax.numpy as jnp
from jax import lax
from jax.experimental import pallas as pl
from jax.experimental.pallas import tpu as pltpu

D_MODEL = 1024
PAGE_SIZE = 128
HEAD_DIM = 64
MOBA_HEADS = 8
MOBA_W = MOBA_HEADS * HEAD_DIM
DIFF_HEADS = 4
DIFF_VDIM = 2 * HEAD_DIM
DIFF_W = DIFF_HEADS * DIFF_VDIM
N_SUB = 2 * DIFF_HEADS
MIX_W = MOBA_W + DIFF_W
MOBA_BLOCK = 256
MOBA_TOPK = 3
N_BUCKETS = 32
REL_MAX_DIST = 128
N_GROUPS = 4
EXPERTS_PER_GROUP = 8
N_EXPERTS = N_GROUPS * EXPERTS_PER_GROUP
D_EXPERT = D_MODEL // 2
D_PLE = 256
LN_EPS = 1e-5
ATTN_SCALE = HEAD_DIM ** -0.5
LANES = 128
SUBLANES = 8
PAGES_PER_STEP = 4
F32 = jnp.float32
BF16 = jnp.bfloat16
HIGHEST = lax.Precision.HIGHEST
NEG_INF = float("-inf")
M_INIT = -1e30
VMEM_LIMIT = 48 * 1024 * 1024
T = MOBA_BLOCK


def _cparams(sem):
    return pltpu.CompilerParams(dimension_semantics=sem, vmem_limit_bytes=VMEM_LIMIT)


def _dot(a, b, precise):
    if precise:
        return jnp.dot(a, b, precision=HIGHEST, preferred_element_type=F32)
    return jnp.dot(a.astype(BF16), b.astype(BF16), preferred_element_type=F32)


def _split2(x):
    hi = x.astype(BF16)
    return hi, (x - hi.astype(F32)).astype(BF16)


def _dot3(a_hi, a_lo, b_hi, b_lo):
    return (jnp.dot(a_hi, b_hi, preferred_element_type=F32) + jnp.dot(a_hi, b_lo, preferred_element_type=F32)
            + jnp.dot(a_lo, b_hi, preferred_element_type=F32))


def _rel_bucket(rel):
    n = jnp.maximum(rel, 0)
    max_exact = N_BUCKETS // 2
    large = max_exact + (jnp.log(jnp.maximum(n, 1).astype(F32) / max_exact)
                         / math.log(REL_MAX_DIST / max_exact) * (N_BUCKETS - max_exact)).astype(jnp.int32)
    large = jnp.minimum(large, N_BUCKETS - 1)
    return jnp.where(n < max_exact, n, large)


def _bias_lookup(tab, rel):
    onehot = (_rel_bucket(rel)[..., None] == jnp.arange(N_BUCKETS)).astype(F32)
    return jnp.einsum('hb,...b->h...', tab, onehot, precision=HIGHEST)


def _layer_norm(x, g, b):
    xc = x - jnp.mean(x, -1, keepdims=True)
    var = jnp.mean(xc * xc, -1, keepdims=True)
    return xc * lax.rsqrt(var + LN_EPS) * g + b


def _fold8(x):
    return jnp.sum(x.reshape(x.shape[0] // SUBLANES, SUBLANES, x.shape[1]), axis=0)


def _colmax(x):
    part = jnp.max(x.reshape(x.shape[0] // SUBLANES, SUBLANES, x.shape[1]), axis=0)
    return jnp.max(part, axis=0, keepdims=True)


def _qkv_prompt_kernel(x_ref, w_ref, k_ref, v_ref, qt_ref, kb_ref, vt_ref, kmean_ref):
    x = x_ref[...].astype(BF16)
    q = jnp.dot(x, w_ref[:, :MIX_W], preferred_element_type=F32)
    qt_ref[...] = q.T.astype(BF16)
    k = jnp.dot(x, w_ref[:, MIX_W:2 * MIX_W], preferred_element_type=F32)
    k_ref[...] = k
    kb_ref[...] = k.astype(BF16)
    kmean_ref[...] = jnp.sum(k[:, :MOBA_W], axis=0, keepdims=True) * (1.0 / MOBA_BLOCK)
    v = jnp.dot(x, w_ref[:, 2 * MIX_W:], preferred_element_type=F32)
    v_ref[...] = v
    vt_ref[...] = v.T.astype(BF16)


def _qkv_prompt(x, w_bf16):
    m = x.shape[0]
    nt = m // T
    row = lambda i: (i, 0)
    tile3 = lambda i: (i, 0, 0)
    return pl.pallas_call(
        _qkv_prompt_kernel,
        grid=(nt,),
        in_specs=[pl.BlockSpec((T, D_MODEL), row),
                  pl.BlockSpec((D_MODEL, 3 * MIX_W), lambda i: (0, 0))],
        out_specs=[pl.BlockSpec((T, MIX_W), row), pl.BlockSpec((T, MIX_W), row),
                   pl.BlockSpec((None, MIX_W, T), tile3), pl.BlockSpec((T, MIX_W), row),
                   pl.BlockSpec((None, MIX_W, T), tile3),
                   pl.BlockSpec((None, 1, MOBA_W), tile3)],
        out_shape=[jax.ShapeDtypeStruct((m, MIX_W), F32), jax.ShapeDtypeStruct((m, MIX_W), F32),
                   jax.ShapeDtypeStruct((nt, MIX_W, T), BF16), jax.ShapeDtypeStruct((m, MIX_W), BF16),
                   jax.ShapeDtypeStruct((nt, MIX_W, T), BF16),
                   jax.ShapeDtypeStruct((nt, 1, MOBA_W), F32)],
        compiler_params=_cparams(("parallel",)),
        name="qkv_prompt",
    )(x, w_bf16)


def _mm_precise_kernel(x_ref, w_ref, o_ref):
    o_ref[...] = jnp.dot(x_ref[...], w_ref[...], precision=HIGHEST, preferred_element_type=F32)


def _mm_precise(x, w, tn):
    m, k = x.shape
    n = w.shape[1]
    return pl.pallas_call(
        _mm_precise_kernel,
        grid=(n // tn,),
        in_specs=[pl.BlockSpec((m, k), lambda j: (0, 0)), pl.BlockSpec((k, tn), lambda j: (0, j))],
        out_specs=pl.BlockSpec((m, tn), lambda j: (0, j)),
        out_shape=jax.ShapeDtypeStruct((m, n), F32),
        compiler_params=_cparams(("parallel",)),
        name="mm_precise",
    )(x, w)


def _stream_blocks(i, n_streams, scores, values, keep_row, bias_d, bias_p,
                   m_ref, l_ref, acc_ref, s_ref, p_ref, dv):
    m_ref[...] = jnp.full(m_ref.shape, M_INIT, F32)
    l_ref[...] = jnp.zeros(l_ref.shape, F32)
    acc_ref[...] = jnp.zeros(acc_ref.shape, F32)

    def block(j, finish_scores, shift_of):
        col_max = []
        for r in range(n_streams):
            s = finish_scores(r, j, scores(r, j))
            s_ref[r] = s
            col_max.append(_colmax(s))
        alphas = []
        for r in range(n_streams):
            shift = shift_of(r)
            m_old = m_ref[r]
            m_new = jnp.maximum(m_old, col_max[r] + shift)
            p = jnp.exp(s_ref[r] - (m_new - shift))
            alpha = jnp.exp(m_old - m_new)
            l_ref[r] = alpha * l_ref[r] + _fold8(p)
            p_ref[r] = p.astype(p_ref.dtype)
            m_ref[r] = m_new
            alphas.append(alpha)
        for r in range(n_streams):
            rows = slice(r * dv, (r + 1) * dv)
            acc_ref[rows, :] = alphas[r] * acc_ref[rows, :] + jnp.dot(values(r, j), p_ref[r],
                                                                      preferred_element_type=F32)

    def keep(r, j, s):
        k = keep_row(r, j)
        return s if k is None else jnp.where(k > 0.5, s, NEG_INF)

    def far_body(j, carry):
        block(j, keep, lambda r: bias_p(r)[0:1, 0:1])
        return carry

    lax.fori_loop(0, jnp.maximum(i - 1, 0), far_body, 0)

    prev_shift = jnp.where(i >= 1, 0.0, NEG_INF).astype(F32).reshape(1, 1)
    block(jnp.maximum(i - 1, 0), lambda r, j, s: keep(r, j, s) + bias_p(r)[...], lambda r: prev_shift)

    key = lax.broadcasted_iota(jnp.int32, (T, T), 0)
    qry = lax.broadcasted_iota(jnp.int32, (T, T), 1)
    zero = jnp.zeros((1, 1), F32)
    block(i, lambda r, j, s: jnp.where(qry >= key, s + bias_d(r)[...], NEG_INF), lambda r: zero)


def _masked_half(qt_slab, half):
    row = lax.broadcasted_iota(jnp.int32, qt_slab.shape, 0)
    mine = (row >= half * HEAD_DIM) & (row < (half + 1) * HEAD_DIM)
    return jnp.where(mine, qt_slab, jnp.zeros_like(qt_slab))


def _top_blocks_t(gate_t, n_valid):
    nb = gate_t.shape[0]
    ids = lax.broadcasted_iota(jnp.int32, gate_t.shape, 0).astype(F32)
    g = jnp.where(ids < n_valid.astype(F32), gate_t, NEG_INF)
    sel = jnp.zeros(gate_t.shape, F32)
    for _ in range(MOBA_TOPK):
        mx = jnp.max(g, axis=0, keepdims=True)
        cand = jnp.where(g == mx, jnp.where(mx > NEG_INF, ids, float(nb)), float(nb))
        pick = ids == jnp.min(cand, axis=0, keepdims=True)
        sel = jnp.where(pick, 1.0, sel)
        g = jnp.where(pick, NEG_INF, g)
    return sel


def _moba_prompt_kernel(qt_ref, k_ref, vt_ref, kmean_ref, bd_ref, bp_ref, o_ref,
                        qs_ref, sel_ref, m_ref, l_ref, acc_ref, s_ref, p_ref):
    i = pl.program_id(1)
    for h in range(MOBA_HEADS):
        slab = slice((h // 2) * LANES, (h // 2 + 1) * LANES)
        qh = _masked_half(qt_ref[slab, :], h % 2)
        gate_t = jnp.dot(kmean_ref[:, slab].astype(BF16), qh, preferred_element_type=F32)
        sel_ref[h] = _top_blocks_t(gate_t, i)
        qs_ref[h] = qh * ATTN_SCALE

    def scores(h, j):
        rows = pl.ds(pl.multiple_of(j * T, T), T)
        slab = slice((h // 2) * LANES, (h // 2 + 1) * LANES)
        return jnp.dot(k_ref[rows, slab], qs_ref[h], preferred_element_type=F32)

    _stream_blocks(i, MOBA_HEADS, scores,
                   lambda h, j: vt_ref[j, h * HEAD_DIM:(h + 1) * HEAD_DIM, :],
                   lambda h, j: sel_ref[h, pl.ds(j, 1), :],
                   lambda h: bd_ref.at[h], lambda h: bp_ref.at[h], m_ref, l_ref, acc_ref, s_ref, p_ref, HEAD_DIM)
    for h in range(MOBA_HEADS):
        rows = slice(h * HEAD_DIM, (h + 1) * HEAD_DIM)
        l = jnp.sum(l_ref[h], axis=0, keepdims=True)
        o_ref[rows, :] = (acc_ref[rows, :] / l).astype(o_ref.dtype)


def _lambda(lam_ref, lam_init):
    lv = lam_ref[...]
    s1 = jnp.sum(lv[0:1] * lv[1:2], axis=-1, keepdims=True)
    s2 = jnp.sum(lv[2:3] * lv[3:4], axis=-1, keepdims=True)
    return jnp.exp(s1) - jnp.exp(s2) + lam_init


def _diff_prompt_kernel(qt_ref, k_ref, vt_ref, bd_ref, bp_ref, lam_ref, g_ref, o_ref,
                        qs_ref, m_ref, l_ref, acc_ref, s_ref, p_ref, *, lam_init):
    i = pl.program_id(1)
    for r in range(N_SUB):
        slab = slice((r // 2) * LANES, (r // 2 + 1) * LANES)
        qs_ref[r] = _masked_half(qt_ref[slab, :], r % 2) * ATTN_SCALE

    def scores(r, j):
        rows = pl.ds(pl.multiple_of(j * T, T), T)
        slab = slice((r // 2) * LANES, (r // 2 + 1) * LANES)
        return jnp.dot(k_ref[rows, slab], qs_ref[r], preferred_element_type=F32)

    _stream_blocks(i, N_SUB, scores,
                   lambda r, j: vt_ref[j, (r // 2) * DIFF_VDIM:(r // 2 + 1) * DIFF_VDIM, :],
                   lambda r, j: None,
                   lambda r: bd_ref.at[r // 2], lambda r: bp_ref.at[r // 2], m_ref, l_ref, acc_ref, s_ref, p_ref,
                   DIFF_VDIM)
    lam = _lambda(lam_ref, lam_init)
    for h in range(DIFF_HEADS):
        parts = []
        for c in range(2):
            r = 2 * h + c
            l = jnp.sum(l_ref[r], axis=0, keepdims=True)
            parts.append(acc_ref[r * DIFF_VDIM:(r + 1) * DIFF_VDIM, :] / l)
        o = parts[0] - lam * parts[1]
        o = o * lax.rsqrt(jnp.mean(o * o, axis=0, keepdims=True) + LN_EPS) * g_ref[...] * (1.0 - lam_init)
        o_ref[h * DIFF_VDIM:(h + 1) * DIFF_VDIM, :] = o.astype(o_ref.dtype)


def _prompt_bias_tiles(tab):
    key = jnp.arange(T)[:, None]
    qry = jnp.arange(T)[None, :]
    return _bias_lookup(tab, qry - key), _bias_lookup(tab, qry - key + T)


def _moba_prompt(qt, kb, vt, kmean, tab_m, b, s):
    nq = s // T
    bd, bp = _prompt_bias_tiles(tab_m)
    const3 = lambda bi, i: (0, 0, 0)
    return pl.pallas_call(
        _moba_prompt_kernel,
        grid=(b, nq),
        in_specs=[pl.BlockSpec((None, MOBA_W, T), lambda bi, i: (bi * nq + i, 0, 0)),
                  pl.BlockSpec((s, MOBA_W), lambda bi, i: (bi, 0)),
                  pl.BlockSpec((nq, MOBA_W, T), lambda bi, i: (bi, 0, 0)),
                  pl.BlockSpec((None, nq, MOBA_W), lambda bi, i: (bi, 0, 0)),
                  pl.BlockSpec((MOBA_HEADS, T, T), const3),
                  pl.BlockSpec((MOBA_HEADS, T, T), const3)],
        out_specs=pl.BlockSpec((None, MOBA_W, T), lambda bi, i: (bi * nq + i, 0, 0)),
        out_shape=jax.ShapeDtypeStruct((b * nq, MOBA_W, T), BF16),
        scratch_shapes=[pltpu.VMEM((MOBA_HEADS, LANES, T), BF16), pltpu.VMEM((MOBA_HEADS, nq, T), F32),
                        pltpu.VMEM((MOBA_HEADS, 1, T), F32), pltpu.VMEM((MOBA_HEADS, SUBLANES, T), F32),
                        pltpu.VMEM((MOBA_W, T), F32),
                        pltpu.VMEM((MOBA_HEADS, T, T), F32), pltpu.VMEM((MOBA_HEADS, T, T), BF16)],
        compiler_params=_cparams(("parallel", "arbitrary")),
        name="moba_prompt",
    )(qt, kb, vt, kmean.reshape(b, nq, MOBA_W), bd, bp)


def _diff_prompt(qt, kb, vt, tab_d, lam_vecs, subln_g, lam_init, b, s):
    nq = s // T
    bd, bp = _prompt_bias_tiles(tab_d)
    g_cols = jnp.broadcast_to(subln_g.reshape(DIFF_VDIM, 1), (DIFF_VDIM, T))
    const3 = lambda bi, i: (0, 0, 0)
    const2 = lambda bi, i: (0, 0)
    return pl.pallas_call(
        functools.partial(_diff_prompt_kernel, lam_init=lam_init),
        grid=(b, nq),
        in_specs=[pl.BlockSpec((None, DIFF_W, T), lambda bi, i: (bi * nq + i, 1, 0)),
                  pl.BlockSpec((s, DIFF_W), lambda bi, i: (bi, 1)),
                  pl.BlockSpec((nq, DIFF_W, T), lambda bi, i: (bi, 1, 0)),
                  pl.BlockSpec((DIFF_HEADS, T, T), const3),
                  pl.BlockSpec((DIFF_HEADS, T, T), const3),
                  pl.BlockSpec((4, HEAD_DIM), const2),
                  pl.BlockSpec((DIFF_VDIM, T), const2)],
        out_specs=pl.BlockSpec((None, DIFF_W, T), lambda bi, i: (bi * nq + i, 0, 0)),
        out_shape=jax.ShapeDtypeStruct((b * nq, DIFF_W, T), BF16),
        scratch_shapes=[pltpu.VMEM((N_SUB, LANES, T), BF16),
                        pltpu.VMEM((N_SUB, 1, T), F32), pltpu.VMEM((N_SUB, SUBLANES, T), F32),
                        pltpu.VMEM((N_SUB * DIFF_VDIM, T), F32),
                        pltpu.VMEM((N_SUB, T, T), F32), pltpu.VMEM((N_SUB, T, T), BF16)],
        compiler_params=_cparams(("parallel", "arbitrary")),
        name="diff_prompt",
    )(qt, kb, vt, bd, bp, lam_vecs, g_cols)


def _sample_main_kernel(pt_ref, q_ref, knew_ref, vnew_ref, *refs, lam_init):
    del pt_ref
    k_refs = refs[:PAGES_PER_STEP]
    v_refs = refs[PAGES_PER_STEP:2 * PAGES_PER_STEP]
    (tail_ref, bias0_ref, lam_ref, g_ref, o_ref, sel_ref,
     qhi_ref, qlo_ref, gacc_ref, m_ref, l_ref, acc_ref) = refs[2 * PAGES_PER_STEP:]
    j = pl.program_id(1)
    nstep = pl.num_programs(1)
    ppb = MOBA_BLOCK // PAGE_SIZE

    @pl.when(j == 0)
    def _():
        row = lax.broadcasted_iota(jnp.int32, (LANES, DIFF_W), 0)
        sub_of_lane = lax.broadcasted_iota(jnp.int32, (LANES, DIFF_W), 1) // HEAD_DIM
        qm = jnp.where(row == sub_of_lane, jnp.broadcast_to(q_ref[:, MOBA_W:], (LANES, DIFF_W)), 0.0)
        qhi_ref[...], qlo_ref[...] = _split2((qm * ATTN_SCALE).T)
        m_ref[...] = jnp.full(m_ref.shape, M_INIT, F32)
        l_ref[...] = jnp.zeros(l_ref.shape, F32)
        acc_ref[...] = jnp.zeros(acc_ref.shape, F32)

    far = tail_ref[0:1, :]
    ksum = None
    scores = []
    for pg in range(PAGES_PER_STEP):
        k_ref = k_refs[pg]
        part = jnp.sum(k_ref[:, :MOBA_W], axis=0, keepdims=True)
        ksum = part if pg % ppb == 0 else ksum + part
        if pg % ppb == ppb - 1:
            blk = j * (PAGES_PER_STEP // ppb) + pg // ppb
            gacc_ref[pl.ds(blk, 1), :] = ksum * (1.0 / MOBA_BLOCK) * q_ref[:, :MOBA_W]
        s = _dot3(*_split2(k_ref[:, MOBA_W:]), qhi_ref[...], qlo_ref[...])
        tail_pg = pg - (PAGES_PER_STEP - ppb)
        if tail_pg >= 0:
            bias = jnp.where(j == nstep - 1, tail_ref[tail_pg * PAGE_SIZE:(tail_pg + 1) * PAGE_SIZE, :], far)
        else:
            bias = far
        scores.append(s + bias)
    m_old = m_ref[...]
    m_new = m_old
    for s in scores:
        m_new = jnp.maximum(m_new, _colmax(s))
    alpha = jnp.exp(m_old - m_new)
    probs = [jnp.exp(s - m_new) for s in scores]
    l_ref[...] = alpha * l_ref[...] + sum(_fold8(p) for p in probs)
    m_ref[...] = m_new
    for r in range(N_SUB):
        vcols = slice((r // 2) * DIFF_VDIM, (r // 2 + 1) * DIFF_VDIM)
        pv = sum(_fold8(p[:, r:r + 1] * v_ref[:, vcols]) for p, v_ref in zip(probs, v_refs))
        acc_ref[r] = alpha[:, r:r + 1] * acc_ref[r] + pv

    @pl.when(j == nstep - 1)
    def _():
        kd = jnp.broadcast_to(knew_ref[:, MOBA_W:], (SUBLANES, DIFF_W))
        s_self = _dot3(*_split2(kd), qhi_ref[...], qlo_ref[...])[0:1] + bias0_ref[...]
        m_old = m_ref[...]
        m_new = jnp.maximum(m_old, s_self)
        alpha = jnp.exp(m_old - m_new)
        p_self = jnp.exp(s_self - m_new)
        l = alpha * jnp.sum(l_ref[...], axis=0, keepdims=True) + p_self
        lam = _lambda(lam_ref, lam_init)
        for h in range(DIFF_HEADS):
            vcols = slice(h * DIFF_VDIM, (h + 1) * DIFF_VDIM)
            parts = []
            for c in range(2):
                r = 2 * h + c
                num = (alpha[:, r:r + 1] * jnp.sum(acc_ref[r], axis=0, keepdims=True)
                       + p_self[:, r:r + 1] * vnew_ref[:, MOBA_W + h * DIFF_VDIM:MOBA_W + (h + 1) * DIFF_VDIM])
                parts.append(num / l[:, r:r + 1])
            oh = parts[0] - lam * parts[1]
            o_ref[:, vcols] = (oh * lax.rsqrt(jnp.mean(oh * oh, -1, keepdims=True) + LN_EPS)
                               * g_ref[...] * (1.0 - lam_init))
        head_row = lax.broadcasted_iota(jnp.int32, (MOBA_HEADS, MOBA_W), 0)
        head_of_lane = lax.broadcasted_iota(jnp.int32, (MOBA_HEADS, MOBA_W), 1) // HEAD_DIM
        seg = jnp.where(head_row == head_of_lane, 1.0, 0.0)
        gate = lax.dot_general(seg, gacc_ref[...], (((1,), (1,)), ((), ())), precision=HIGHEST,
                               preferred_element_type=F32)
        ids = lax.broadcasted_iota(jnp.int32, gate.shape, 1).astype(F32)
        out_lane = lax.broadcasted_iota(jnp.int32, (MOBA_HEADS, LANES), 1)
        picked = jnp.zeros((MOBA_HEADS, LANES), F32)
        for t in range(MOBA_TOPK):
            mx = jnp.max(gate, axis=-1, keepdims=True)
            idx = jnp.min(jnp.where(gate == mx, ids, float(gate.shape[-1])), axis=-1, keepdims=True)
            picked = jnp.where(out_lane == t, idx, picked)
            gate = jnp.where(ids == idx, NEG_INF, gate)
        sel_ref[...] = picked.astype(jnp.int32)


def _sample_main(page_table, q_s, k_s, v_s, cache_k, cache_v, layer, tab_d, lam_vecs, subln_g, lam_init):
    db, n_pages = page_table.shape
    past = n_pages * PAGE_SIZE
    assert n_pages % PAGES_PER_STEP == 0 and past // MOBA_BLOCK >= MOBA_TOPK
    nblk = past // MOBA_BLOCK
    nstep = n_pages // PAGES_PER_STEP
    tab_sub = jnp.repeat(tab_d, 2, axis=0)
    tail = _bias_lookup(tab_sub, MOBA_BLOCK - jnp.arange(MOBA_BLOCK)).T
    tail = jnp.pad(tail, ((0, 0), (0, LANES - N_SUB)))
    bias_self = jnp.pad(tab_sub[:, 0].reshape(1, N_SUB), ((0, 0), (0, LANES - N_SUB)))

    def page(pg, col):
        return lambda b, j, pt: (pt[b * n_pages + PAGES_PER_STEP * j + pg], layer, 0, col)

    tok = lambda b, j, pt: (b, 0, 0)
    const2 = lambda b, j, pt: (0, 0)
    grid_spec = pltpu.PrefetchScalarGridSpec(
        num_scalar_prefetch=1,
        grid=(db, nstep),
        in_specs=([pl.BlockSpec((None, 1, MIX_W), tok)] * 3
                  + [pl.BlockSpec((None, None, PAGE_SIZE, MIX_W), page(pg, 0)) for pg in range(PAGES_PER_STEP)]
                  + [pl.BlockSpec((None, None, PAGE_SIZE, DIFF_W), page(pg, 1)) for pg in range(PAGES_PER_STEP)]
                  + [pl.BlockSpec((MOBA_BLOCK, LANES), const2), pl.BlockSpec((1, LANES), const2),
                     pl.BlockSpec((4, HEAD_DIM), const2), pl.BlockSpec((1, DIFF_VDIM), const2)]),
        out_specs=[pl.BlockSpec((None, 1, DIFF_W), tok),
                   pl.BlockSpec((None, MOBA_HEADS, LANES), tok)],
        scratch_shapes=[pltpu.VMEM((DIFF_W, LANES), BF16), pltpu.VMEM((DIFF_W, LANES), BF16),
                        pltpu.VMEM((nblk, MOBA_W), F32),
                        pltpu.VMEM((1, LANES), F32), pltpu.VMEM((SUBLANES, LANES), F32),
                        pltpu.VMEM((N_SUB, SUBLANES, DIFF_VDIM), F32)],
    )
    o_d, sel = pl.pallas_call(
        functools.partial(_sample_main_kernel, lam_init=lam_init),
        grid_spec=grid_spec,
        out_shape=[jax.ShapeDtypeStruct((db, 1, DIFF_W), F32),
                   jax.ShapeDtypeStruct((db, MOBA_HEADS, LANES), jnp.int32)],
        compiler_params=_cparams(("parallel", "arbitrary")),
        name="sample_main",
    )(page_table.reshape(-1), q_s, k_s, v_s, *([cache_k] * PAGES_PER_STEP), *([cache_v] * PAGES_PER_STEP),
      tail, bias_self, lam_vecs, subln_g)
    return o_d.reshape(db, DIFF_W), sel[:, :, :MOBA_TOPK]


def _sample_moba_kernel(pt_ref, sel_ref, q_ref, knew_ref, vnew_ref, *refs):
    del pt_ref, sel_ref
    n_pg = MOBA_TOPK * (MOBA_BLOCK // PAGE_SIZE)
    k_refs = refs[:n_pg]
    v_refs = refs[n_pg:2 * n_pg]
    bias_refs = refs[2 * n_pg:2 * n_pg + MOBA_TOPK]
    bias0_ref, o_ref = refs[2 * n_pg + MOBA_TOPK:]
    h = pl.program_id(1)
    par = h % 2
    lane = lax.broadcasted_iota(jnp.int32, (1, LANES), 1)
    mine = (lane >= par * HEAD_DIM) & (lane < (par + 1) * HEAD_DIM)
    qm = jnp.where(mine, q_ref[...], 0.0) * ATTN_SCALE
    q_hi, q_lo = _split2(jnp.broadcast_to(qm, (LANES, LANES)).T)
    m = jnp.full((1, LANES), M_INIT, F32)
    l8 = jnp.zeros((SUBLANES, LANES), F32)
    acc = jnp.zeros((SUBLANES, LANES), F32)
    ppb = MOBA_BLOCK // PAGE_SIZE
    for pg in range(n_pg):
        brow = bias_refs[pg // ppb][:, (pg % ppb) * PAGE_SIZE:(pg % ppb + 1) * PAGE_SIZE]
        s = _dot3(*_split2(k_refs[pg][...]), q_hi, q_lo)
        s = s + jnp.broadcast_to(brow, (PAGE_SIZE, PAGE_SIZE)).T
        m_new = jnp.maximum(m, _colmax(s))
        p = jnp.exp(s - m_new)
        alpha = jnp.exp(m - m_new)
        l8 = alpha * l8 + _fold8(p)
        acc = alpha * acc + _fold8(p * v_refs[pg][...])
        m = m_new
    s_self = jnp.sum(qm * knew_ref[...], axis=-1, keepdims=True) + bias0_ref[:, 0:1]
    m_new = jnp.maximum(m, s_self)
    alpha = jnp.exp(m - m_new)
    p_self = jnp.exp(s_self - m_new)
    o = ((alpha * jnp.sum(acc, axis=0, keepdims=True) + p_self * vnew_ref[...])
         / (alpha * jnp.sum(l8, axis=0, keepdims=True) + p_self))
    o_ref[...] = jnp.where(par == 0, o[:, :HEAD_DIM], o[:, HEAD_DIM:])


def _sample_moba(page_table, sel, q_s, k_s, v_s, cache_k, cache_v, layer, tab_m):
    db, n_pages = page_table.shape
    past = n_pages * PAGE_SIZE
    ppb = MOBA_BLOCK // PAGE_SIZE
    n_pg = MOBA_TOPK * ppb
    bias_past = _bias_lookup(tab_m, past - jnp.arange(past)).reshape(MOBA_HEADS, 1, past)
    bias_self = jnp.broadcast_to(tab_m[:, 0:1, None], (MOBA_HEADS, 1, LANES))

    def sel_at(b, h, t, sl):
        return sl[(b * MOBA_HEADS + h) * MOBA_TOPK + t]

    def page(pg):
        return lambda b, h, pt, sl: (pt[b * n_pages + ppb * sel_at(b, h, pg // ppb, sl) + pg % ppb], layer, 0, h // 2)

    def bias_blk(t):
        return lambda b, h, pt, sl: (h, 0, sel_at(b, h, t, sl))

    tok = lambda b, h, pt, sl: (b, 0, h // 2)
    grid_spec = pltpu.PrefetchScalarGridSpec(
        num_scalar_prefetch=2,
        grid=(db, MOBA_HEADS),
        in_specs=([pl.BlockSpec((None, 1, LANES), tok)] * 3
                  + [pl.BlockSpec((None, None, PAGE_SIZE, LANES), page(pg)) for pg in range(n_pg)] * 2
                  + [pl.BlockSpec((None, 1, MOBA_BLOCK), bias_blk(t)) for t in range(MOBA_TOPK)]
                  + [pl.BlockSpec((None, 1, LANES), lambda b, h, pt, sl: (h, 0, 0))]),
        out_specs=pl.BlockSpec((None, None, 1, HEAD_DIM), lambda b, h, pt, sl: (b, h, 0, 0)),
    )
    o_m = pl.pallas_call(
        _sample_moba_kernel,
        grid_spec=grid_spec,
        out_shape=jax.ShapeDtypeStruct((db, MOBA_HEADS, 1, HEAD_DIM), F32),
        compiler_params=_cparams(("parallel", "arbitrary")),
        name="sample_moba",
    )(page_table.reshape(-1), sel.reshape(-1), q_s, k_s, v_s,
      *([cache_k] * n_pg), *([cache_v] * n_pg), *([bias_past] * MOBA_TOPK), bias_self)
    return o_m.reshape(db, MOBA_W)


def _post_attn_kernel(h_ref, mm_ref, md_ref, wo_ref, g_ref, b_ref, wr_ref, br_ref,
                      a_ref, eid_ref, gate_ref, *, alpha, precise, transposed):
    if transposed:
        mm = mm_ref[...].astype(F32).T
        md = md_ref[...].astype(F32).T
    else:
        mm, md = mm_ref[...], md_ref[...]
    mix = _dot(mm, wo_ref[:MOBA_W, :], precise) + _dot(md, wo_ref[MOBA_W:, :], precise)
    a = _layer_norm(alpha * h_ref[...] + mix, g_ref[...], b_ref[...])
    a_ref[...] = a
    logits = jnp.dot(a, wr_ref[...], precision=HIGHEST, preferred_element_type=F32) + br_ref[...]
    lane = lax.broadcasted_iota(jnp.int32, logits.shape, 1).astype(F32)
    none = float(LANES)
    gl = jnp.where(lane < N_GROUPS, logits, NEG_INF)
    gmax = jnp.max(gl, axis=-1, keepdims=True)
    g_gate = 1.0 / jnp.sum(jnp.exp(gl - gmax), axis=-1, keepdims=True)
    g_idx = jnp.min(jnp.where(gl == gmax, lane, none), axis=-1, keepdims=True)
    lo = N_GROUPS + EXPERTS_PER_GROUP * g_idx
    el = jnp.where(lane >= lo, jnp.where(lane < lo + EXPERTS_PER_GROUP, logits, NEG_INF), NEG_INF)
    m1 = jnp.max(el, axis=-1, keepdims=True)
    i1 = jnp.min(jnp.where(el == m1, lane, none), axis=-1, keepdims=True)
    el2 = jnp.where(lane == i1, NEG_INF, el)
    m2 = jnp.max(el2, axis=-1, keepdims=True)
    i2 = jnp.min(jnp.where(el2 == m2, lane, none), axis=-1, keepdims=True)
    r = jnp.exp(m2 - m1)
    w1 = g_gate / (1.0 + r)
    w2 = g_gate * r / (1.0 + r)
    eid_ref[...] = jnp.where(lane == 0.0, i1 - N_GROUPS, jnp.where(lane == 1.0, i2 - N_GROUPS, 0.0)).astype(jnp.int32)
    gate_ref[...] = jnp.where(lane == 0.0, w1, jnp.where(lane == 1.0, w2, 0.0))


def _post_attn(h, mix_m, mix_d, w_out, ln_g, ln_b, w_router, b_router, alpha, precise, tm, transposed):
    m = h.shape[0]
    row = lambda i: (i, 0)
    const = lambda i: (0, 0)
    if transposed:
        assert tm == T
        mix_specs = [pl.BlockSpec((None, MOBA_W, T), lambda i: (i, 0, 0)),
                     pl.BlockSpec((None, DIFF_W, T), lambda i: (i, 0, 0))]
    else:
        mix_specs = [pl.BlockSpec((tm, MOBA_W), row), pl.BlockSpec((tm, DIFF_W), row)]
    return pl.pallas_call(
        functools.partial(_post_attn_kernel, alpha=alpha, precise=precise, transposed=transposed),
        grid=(m // tm,),
        in_specs=[pl.BlockSpec((tm, D_MODEL), row)] + mix_specs + [
            pl.BlockSpec((MIX_W, D_MODEL), const),
            pl.BlockSpec((1, D_MODEL), const), pl.BlockSpec((1, D_MODEL), const),
            pl.BlockSpec((D_MODEL, LANES), const), pl.BlockSpec((1, LANES), const)],
        out_specs=[pl.BlockSpec((tm, D_MODEL), row), pl.BlockSpec((tm, LANES), row),
                   pl.BlockSpec((tm, LANES), row)],
        out_shape=[jax.ShapeDtypeStruct((m, D_MODEL), F32), jax.ShapeDtypeStruct((m, LANES), jnp.int32),
                   jax.ShapeDtypeStruct((m, LANES), F32)],
        compiler_params=_cparams(("parallel",)),
        name="post_attn",
    )(h, mix_m, mix_d, w_out, ln_g, ln_b, w_router, b_router)


def _moe_kernel(code_ref, blk_e_ref, blk_src_ref, blk_n_ref, nblk_ref, a_hbm, w1_ref, w3_ref, w2_ref, y_hbm,
                xbuf, ybuf, w1b, w3b, w2b, gsem, ssem, *, rb, n_tok):
    i = pl.program_id(0)
    n_used = nblk_ref[0]

    def start_gather(blk, slot):
        base = blk_src_ref[blk]

        def body(c, carry):
            for u in range(SUBLANES):
                tok = code_ref[base + c * SUBLANES + u] >> 1
                pltpu.make_async_copy(a_hbm.at[pl.ds(tok, 1)], xbuf.at[slot, c, pl.ds(u, 1)],
                                      gsem.at[slot]).start()
            return carry
        lax.fori_loop(0, rb // SUBLANES, body, 0)

    def start_scatter(blk, slot):
        base = blk_src_ref[blk]
        n_valid = blk_n_ref[blk]

        def body(c, carry):
            for u in range(SUBLANES):
                r = c * SUBLANES + u
                code = code_ref[base + r]
                dst = jnp.where(r < n_valid, (code >> 1) + (code & 1) * n_tok, 2 * n_tok + slot * rb + r)
                pltpu.make_async_copy(ybuf.at[slot, c, pl.ds(u, 1)], y_hbm.at[pl.ds(dst, 1)],
                                      ssem.at[slot]).start()
            return carry
        lax.fori_loop(0, rb // SUBLANES, body, 0)

    def wait_gather(slot):
        pltpu.make_async_copy(xbuf.at[slot], xbuf.at[slot], gsem.at[slot]).wait()

    def wait_scatter(slot):
        pltpu.make_async_copy(ybuf.at[slot], ybuf.at[slot], ssem.at[slot]).wait()

    @pl.when(i < n_used)
    def _():
        slot = i % 2

        @pl.when(i == 0)
        def _():
            start_gather(0, 0)
            ybuf[1] = jnp.zeros(ybuf.shape[1:], F32)
            for half in range(2):
                def zero_rows(c, carry, half=half):
                    for u in range(SUBLANES):
                        dst = 2 * n_tok + half * rb + c * SUBLANES + u
                        pltpu.make_async_copy(ybuf.at[1, c, pl.ds(u, 1)], y_hbm.at[pl.ds(dst, 1)],
                                              ssem.at[1]).start()
                    return carry
                lax.fori_loop(0, rb // SUBLANES, zero_rows, 0)
                wait_scatter(1)

        @pl.when(i + 1 < n_used)
        def _():
            start_gather(i + 1, 1 - slot)

        e = blk_e_ref[i]

        @pl.when((i == 0) | (e != blk_e_ref[jnp.maximum(i - 1, 0)]))
        def _():
            w1b[...] = w1_ref[...].astype(BF16)
            w3b[...] = w3_ref[...].astype(BF16)
            w2b[...] = w2_ref[...].astype(BF16)

        wait_gather(slot)
        x = xbuf[slot].reshape(rb, D_MODEL).astype(BF16)
        h1 = jnp.dot(x, w1b[...], preferred_element_type=F32)
        h3 = jnp.dot(x, w3b[...], preferred_element_type=F32)
        hidden = (h1 * jax.nn.sigmoid(h1) * h3).astype(BF16)
        y = jnp.dot(hidden, w2b[...], preferred_element_type=F32)

        @pl.when(i >= 2)
        def _():
            wait_scatter(slot)

        ybuf[slot] = y.reshape(rb // SUBLANES, SUBLANES, D_MODEL)
        start_scatter(i, slot)

        @pl.when(i == n_used - 1)
        def _():
            @pl.when(i >= 1)
            def _():
                wait_scatter(1 - slot)
            wait_scatter(slot)


def _route_tables(eid, rb):
    n = eid.shape[0]
    m = n * 2
    e_flat = eid.reshape(m)
    keys = lax.sort(e_flat * m + jnp.arange(m, dtype=jnp.int32))
    code = keys % m
    experts = jnp.arange(N_EXPERTS, dtype=jnp.int32)
    counts = jnp.sum((e_flat[:, None] == experts[None, :]).astype(jnp.int32), axis=0)
    padded = (counts + rb - 1) // rb * rb
    start = jnp.cumsum(counts) - counts
    p_end = jnp.cumsum(padded)
    p_start = p_end - padded
    n_blk = -(-(m + N_EXPERTS * (rb - 1)) // rb)
    blk_row = jnp.arange(n_blk, dtype=jnp.int32) * rb
    blk_e = jnp.minimum(jnp.sum((p_end[None, :] <= blk_row[:, None]).astype(jnp.int32), axis=1), N_EXPERTS - 1)
    onehot = (blk_e[:, None] == experts[None, :]).astype(jnp.int32)
    pick = lambda table: jnp.sum(onehot * table[None, :], axis=1)
    off = blk_row - pick(p_start)
    blk_src = jnp.clip(pick(start) + off, 0, m - 1).astype(jnp.int32)
    blk_n = jnp.clip(pick(counts) - off, 0, rb).astype(jnp.int32)
    n_used = (p_end[-1:] // rb).astype(jnp.int32)
    code = jnp.pad(code.astype(jnp.int32), (0, rb))
    return code, blk_e.astype(jnp.int32), blk_src, blk_n, n_used, n_blk


def _moe(a, eid, w1, w3, w2, rb):
    n = a.shape[0]
    code, blk_e, blk_src, blk_n, n_used, n_blk = _route_tables(eid, rb)
    wmap = lambda i, code, be, bs, bn, nu: (be[i], 0, 0)
    grid_spec = pltpu.PrefetchScalarGridSpec(
        num_scalar_prefetch=5,
        grid=(n_blk,),
        in_specs=[pl.BlockSpec(memory_space=pl.ANY),
                  pl.BlockSpec((None, D_MODEL, D_EXPERT), wmap),
                  pl.BlockSpec((None, D_MODEL, D_EXPERT), wmap),
                  pl.BlockSpec((None, D_EXPERT, D_MODEL), wmap)],
        out_specs=pl.BlockSpec(memory_space=pl.ANY),
        scratch_shapes=[pltpu.VMEM((2, rb // SUBLANES, SUBLANES, D_MODEL), F32),
                        pltpu.VMEM((2, rb // SUBLANES, SUBLANES, D_MODEL), F32),
                        pltpu.VMEM((D_MODEL, D_EXPERT), BF16), pltpu.VMEM((D_MODEL, D_EXPERT), BF16),
                        pltpu.VMEM((D_EXPERT, D_MODEL), BF16),
                        pltpu.SemaphoreType.DMA((2,)), pltpu.SemaphoreType.DMA((2,))],
    )
    return pl.pallas_call(
        functools.partial(_moe_kernel, rb=rb, n_tok=n),
        grid_spec=grid_spec,
        out_shape=jax.ShapeDtypeStruct((2 * n + 2 * rb, D_MODEL), F32),
        compiler_params=_cparams(("arbitrary",)),
        name="moe",
    )(code, blk_e, blk_src, blk_n, n_used, a, w1, w3, w2)


def _finish_kernel(a_ref, y0_ref, y1_ref, gate_ref, p_ref, g_ref, b_ref, wpg_ref, bpg_ref, wple_ref, o_ref,
                   *, alpha, precise):
    gates = gate_ref[...]
    moe = y0_ref[...] * gates[:, 0:1] + y1_ref[...] * gates[:, 1:2]
    c = _layer_norm(alpha * a_ref[...] + moe, g_ref[...], b_ref[...])
    gate = jax.nn.sigmoid(_dot(c, wpg_ref[...], precise) + bpg_ref[...])
    o_ref[...] = c + gate * _dot(p_ref[...], wple_ref[...], precise)


def _finish(a, y, gates, p, ln_g, ln_b, w_pg, b_pg, w_ple, alpha, precise, tm):
    m = a.shape[0]
    assert m % tm == 0
    row = lambda i: (i, 0)
    const = lambda i: (0, 0)
    return pl.pallas_call(
        functools.partial(_finish_kernel, alpha=alpha, precise=precise),
        grid=(m // tm,),
        in_specs=[pl.BlockSpec((tm, D_MODEL), row), pl.BlockSpec((tm, D_MODEL), row),
                  pl.BlockSpec((tm, D_MODEL), lambda i: (m // tm + i, 0)),
                  pl.BlockSpec((tm, LANES), row), pl.BlockSpec((tm, D_PLE), row),
                  pl.BlockSpec((1, D_MODEL), const), pl.BlockSpec((1, D_MODEL), const),
                  pl.BlockSpec((D_MODEL, D_MODEL), const), pl.BlockSpec((1, D_MODEL), const),
                  pl.BlockSpec((D_PLE, D_MODEL), const)],
        out_specs=pl.BlockSpec((tm, D_MODEL), row),
        out_shape=jax.ShapeDtypeStruct((m, D_MODEL), F32),
        compiler_params=_cparams(("parallel",)),
        name="finish",
    )(a, y, y, gates, p, ln_g, ln_b, w_pg, b_pg, w_ple)


def _pack_router(w_group, b_group, w_er, b_er):
    w = jnp.concatenate([w_group, jnp.transpose(w_er, (1, 0, 2)).reshape(D_MODEL, N_EXPERTS)], axis=1)
    b = jnp.concatenate([b_group, b_er.reshape(N_EXPERTS)])
    pad = LANES - w.shape[1]
    return jnp.pad(w, ((0, 0), (0, pad))), jnp.pad(b, (0, pad)).reshape(1, LANES)


def kernel(x_prompt, x_sample, cache_k, cache_v, page_table, p_prompt, p_sample, rel_bias, w_in, w_out, lam_q1, lam_k1, lam_q2, lam_k2, subln_g, ln1_g, ln1_b, w_group, b_group, w_erouter, b_erouter, w1, w3, w2, ln2_g, ln2_b, w_ple, w_pg, b_pg):
    depth = w_in.shape[0]
    b, s, _ = x_prompt.shape
    db, ds, _ = x_sample.shape
    assert ds == 1 and s % T == 0
    alpha = (2 * depth) ** 0.25
    tab_m = rel_bias[:, :MOBA_HEADS].T
    tab_d = rel_bias[:, MOBA_HEADS:].T
    hp = x_prompt.reshape(b * s, D_MODEL)
    hs = x_sample.reshape(db, D_MODEL)
    kp_rows, vp_rows, ks_rows, vs_rows = [], [], [], []
    for i in range(depth):
        lam_init = 0.8 - 0.6 * math.exp(-0.3 * i)
        lam_vecs = jnp.stack([lam_q1[i], lam_k1[i], lam_q2[i], lam_k2[i]]).astype(F32)
        g_sub = subln_g[i].reshape(1, DIFF_VDIM)
        w_router, b_router = _pack_router(w_group[i], b_group[i], w_erouter[i], b_erouter[i])
        ln1 = (ln1_g[i].reshape(1, D_MODEL), ln1_b[i].reshape(1, D_MODEL))
        ln2 = (ln2_g[i].reshape(1, D_MODEL), ln2_b[i].reshape(1, D_MODEL))
        b_pg_i = b_pg[i].reshape(1, D_MODEL)

        k_p, v_p, qt, kb, vt, kmean = _qkv_prompt(hp, w_in[i].astype(BF16))
        mix_m = _moba_prompt(qt, kb, vt, kmean, tab_m, b, s)
        mix_d = _diff_prompt(qt, kb, vt, tab_d, lam_vecs, g_sub, lam_init, b, s)
        a_p, eid_p, gate_p = _post_attn(hp, mix_m, mix_d, w_out[i].astype(BF16), *ln1, w_router, b_router,
                                        alpha, False, T, True)
        y_p = _moe(a_p, eid_p[:, :2], w1[i], w3[i], w2[i], 256)
        hp = _finish(a_p, y_p, gate_p, p_prompt[i].reshape(b * s, D_PLE), *ln2, w_pg[i].astype(BF16), b_pg_i,
                     w_ple[i].astype(BF16), alpha, False, 256)

        qkv_s = _mm_precise(hs, w_in[i], MIX_W)
        q_s = qkv_s[:, :MIX_W].reshape(db, 1, MIX_W)
        k_s = qkv_s[:, MIX_W:2 * MIX_W].reshape(db, 1, MIX_W)
        v_s = qkv_s[:, 2 * MIX_W:].reshape(db, 1, MIX_W)
        o_d, sel = _sample_main(page_table, q_s, k_s, v_s, cache_k, cache_v, i, tab_d, lam_vecs, g_sub, lam_init)
        o_m = _sample_moba(page_table, sel, q_s, k_s, v_s, cache_k, cache_v, i, tab_m)
        a_s, eid_s, gate_s = _post_attn(hs, o_m, o_d, w_out[i], *ln1, w_router, b_router, alpha, True, db, False)
        y_s = _moe(a_s, eid_s[:, :2], w1[i], w3[i], w2[i], 64)
        hs = _finish(a_s, y_s, gate_s, p_sample[i].reshape(db, D_PLE), *ln2, w_pg[i], b_pg_i, w_ple[i],
                     alpha, True, db)

        kp_rows.append(k_p.reshape(b, s, MIX_W))
        vp_rows.append(v_p.reshape(b, s, MIX_W))
        ks_rows.append(k_s.reshape(db, ds, MIX_W))
        vs_rows.append(v_s.reshape(db, ds, MIX_W))
    return (hp.reshape(b, s, D_MODEL), hs.reshape(db, ds, D_MODEL),
            jnp.stack(kp_rows, axis=1), jnp.stack(vp_rows, axis=1),
            jnp.stack(ks_rows, axis=1), jnp.stack(vs_rows, axis=1))
```

```python
import functools
import math

import jax
import jax.numpy as jnp
from jax import lax
from jax.experimental import pallas as pl
from jax.experimental.pallas import tpu as pltpu

D_MODEL = 1024
PAGE_SIZE = 128
HEAD_DIM = 64
MOBA_HEADS = 8
MOBA_W = MOBA_HEADS * HEAD_DIM
DIFF_HEADS = 4
DIFF_VDIM = 2 * HEAD_DIM
DIFF_W = DIFF_HEADS * DIFF_VDIM
N_SUB = 2 * DIFF_HEADS
MIX_W = MOBA_W + DIFF_W
MOBA_BLOCK = 256
MOBA_TOPK = 3
N_BUCKETS = 32
REL_MAX_DIST = 128
N_GROUPS = 4
EXPERTS_PER_GROUP = 8
N_EXPERTS = N_GROUPS * EXPERTS_PER_GROUP
D_EXPERT = D_MODEL // 2
D_PLE = 256
LN_EPS = 1e-5
ATTN_SCALE = HEAD_DIM ** -0.5
LANES = 128
SUBLANES = 8
N_CHUNK = D_MODEL // LANES
PAGES_PER_STEP = 4
F32 = jnp.float32
BF16 = jnp.bfloat16
HIGHEST = lax.Precision.HIGHEST
NEG_INF = float("-inf")
M_INIT = -1e30
VMEM_LIMIT = 48 * 1024 * 1024
T = MOBA_BLOCK


def _cparams(sem):
    return pltpu.CompilerParams(dimension_semantics=sem, vmem_limit_bytes=VMEM_LIMIT)


def _dot(a, b, precise):
    if precise:
        return jnp.dot(a, b, precision=HIGHEST, preferred_element_type=F32)
    return jnp.dot(a.astype(BF16), b.astype(BF16), preferred_element_type=F32)


def _split2(x):
    hi = x.astype(BF16)
    return hi, (x - hi.astype(F32)).astype(BF16)


def _dot3(a_hi, a_lo, b_hi, b_lo):
    return (jnp.dot(a_hi, b_hi, preferred_element_type=F32) + jnp.dot(a_hi, b_lo, preferred_element_type=F32)
            + jnp.dot(a_lo, b_hi, preferred_element_type=F32))


def _rel_bucket(rel):
    n = jnp.maximum(rel, 0)
    max_exact = N_BUCKETS // 2
    large = max_exact + (jnp.log(jnp.maximum(n, 1).astype(F32) / max_exact)
                         / math.log(REL_MAX_DIST / max_exact) * (N_BUCKETS - max_exact)).astype(jnp.int32)
    large = jnp.minimum(large, N_BUCKETS - 1)
    return jnp.where(n < max_exact, n, large)


def _bias_lookup(tab, rel):
    onehot = (_rel_bucket(rel)[..., None] == jnp.arange(N_BUCKETS)).astype(F32)
    return jnp.einsum('hb,...b->h...', tab, onehot, precision=HIGHEST)


def _layer_norm(x, g, b):
    xc = x - jnp.mean(x, -1, keepdims=True)
    var = jnp.mean(xc * xc, -1, keepdims=True)
    return xc * lax.rsqrt(var + LN_EPS) * g + b


def _fold8(x):
    return jnp.sum(x.reshape(x.shape[0] // SUBLANES, SUBLANES, x.shape[1]), axis=0)


def _store_tiles(ref, x):
    rows = x.shape[0]
    for j in range(N_CHUNK):
        ref[pl.ds(j, rows, stride=N_CHUNK), :] = x[:, j * LANES:(j + 1) * LANES]


def _load_tiles(ref):
    rows = ref.shape[0] // N_CHUNK
    return jnp.concatenate([ref[pl.ds(j, rows, stride=N_CHUNK), :] for j in range(N_CHUNK)], axis=1)


def _colmax(x):
    part = jnp.max(x.reshape(x.shape[0] // SUBLANES, SUBLANES, x.shape[1]), axis=0)
    return jnp.max(part, axis=0, keepdims=True)


def _qkv_prompt_kernel(x_ref, w_ref, k_ref, v_ref, qt_ref, kb_ref, vt_ref, kmean_ref):
    x = x_ref[...].astype(BF16)
    q = jnp.dot(x, w_ref[:, :MIX_W], preferred_element_type=F32)
    qt_ref[...] = q.T.astype(BF16)
    k = jnp.dot(x, w_ref[:, MIX_W:2 * MIX_W], preferred_element_type=F32)
    k_ref[...] = k
    kb_ref[...] = k.astype(BF16)
    kmean_ref[...] = jnp.sum(k[:, :MOBA_W], axis=0, keepdims=True) * (1.0 / MOBA_BLOCK)
    v = jnp.dot(x, w_ref[:, 2 * MIX_W:], preferred_element_type=F32)
    v_ref[...] = v
    vt_ref[...] = v.T.astype(BF16)


def _qkv_prompt(x, w_bf16):
    m = x.shape[0]
    nt = m // T
    row = lambda i: (i, 0)
    tile3 = lambda i: (i, 0, 0)
    return pl.pallas_call(
        _qkv_prompt_kernel,
        grid=(nt,),
        in_specs=[pl.BlockSpec((T, D_MODEL), row),
                  pl.BlockSpec((D_MODEL, 3 * MIX_W), lambda i: (0, 0))],
        out_specs=[pl.BlockSpec((T, MIX_W), row), pl.BlockSpec((T, MIX_W), row),
                   pl.BlockSpec((None, MIX_W, T), tile3), pl.BlockSpec((T, MIX_W), row),
                   pl.BlockSpec((None, MIX_W, T), tile3),
                   pl.BlockSpec((None, 1, MOBA_W), tile3)],
        out_shape=[jax.ShapeDtypeStruct((m, MIX_W), F32), jax.ShapeDtypeStruct((m, MIX_W), F32),
                   jax.ShapeDtypeStruct((nt, MIX_W, T), BF16), jax.ShapeDtypeStruct((m, MIX_W), BF16),
                   jax.ShapeDtypeStruct((nt, MIX_W, T), BF16),
                   jax.ShapeDtypeStruct((nt, 1, MOBA_W), F32)],
        compiler_params=_cparams(("parallel",)),
        name="qkv_prompt",
    )(x, w_bf16)


def _mm_precise_kernel(x_ref, w_ref, o_ref):
    o_ref[...] = jnp.dot(x_ref[...], w_ref[...], precision=HIGHEST, preferred_element_type=F32)


def _mm_precise(x, w, tn):
    m, k = x.shape
    n = w.shape[1]
    return pl.pallas_call(
        _mm_precise_kernel,
        grid=(n // tn,),
        in_specs=[pl.BlockSpec((m, k), lambda j: (0, 0)), pl.BlockSpec((k, tn), lambda j: (0, j))],
        out_specs=pl.BlockSpec((m, tn), lambda j: (0, j)),
        out_shape=jax.ShapeDtypeStruct((m, n), F32),
        compiler_params=_cparams(("parallel",)),
        name="mm_precise",
    )(x, w)


def _stream_blocks(i, n_streams, scores, values, keep_row, bias_d, bias_p,
                   m_ref, l_ref, acc_ref, s_ref, p_ref, dv):
    m_ref[...] = jnp.full(m_ref.shape, M_INIT, F32)
    l_ref[...] = jnp.zeros(l_ref.shape, F32)
    acc_ref[...] = jnp.zeros(acc_ref.shape, F32)

    def block(j, finish_scores, shift_of):
        col_max = []
        for r in range(n_streams):
            s = finish_scores(r, j, scores(r, j))
            s_ref[r] = s
            col_max.append(_colmax(s))
        alphas = []
        for r in range(n_streams):
            shift = shift_of(r)
            m_old = m_ref[r]
            m_new = jnp.maximum(m_old, col_max[r] + shift)
            p = jnp.exp(s_ref[r] - (m_new - shift))
            alpha = jnp.exp(m_old - m_new)
            p_ref[r] = p.astype(p_ref.dtype)
            m_ref[r] = m_new
            alphas.append(alpha)
        ones = jnp.ones((2 * SUBLANES, T), p_ref.dtype)
        for r in range(n_streams):
            rows = slice(r * dv, (r + 1) * dv)
            pv = jnp.dot(jnp.concatenate([values(r, j), ones], axis=0), p_ref[r], preferred_element_type=F32)
            acc_ref[rows, :] = alphas[r] * acc_ref[rows, :] + pv[:dv]
            l_ref[r] = alphas[r] * l_ref[r] + pv[dv:dv + 1]

    def keep(r, j, s):
        k = keep_row(r, j)
        return s if k is None else jnp.where(k > 0.5, s, NEG_INF)

    def far_body(j, carry):
        block(j, keep, lambda r: bias_p(r)[0:1, 0:1])
        return carry

    lax.fori_loop(0, jnp.maximum(i - 1, 0), far_body, 0)

    prev_shift = jnp.where(i >= 1, 0.0, NEG_INF).astype(F32).reshape(1, 1)
    block(jnp.maximum(i - 1, 0), lambda r, j, s: keep(r, j, s) + bias_p(r)[...], lambda r: prev_shift)

    key = lax.broadcasted_iota(jnp.int32, (T, T), 0)
    qry = lax.broadcasted_iota(jnp.int32, (T, T), 1)
    zero = jnp.zeros((1, 1), F32)
    block(i, lambda r, j, s: jnp.where(qry >= key, s + bias_d(r)[...], NEG_INF), lambda r: zero)


def _masked_half(qt_slab, half):
    row = lax.broadcasted_iota(jnp.int32, qt_slab.shape, 0)
    mine = (row >= half * HEAD_DIM) & (row < (half + 1) * HEAD_DIM)
    return jnp.where(mine, qt_slab, jnp.zeros_like(qt_slab))


def _top_blocks_t(gate_t, n_valid):
    nb = gate_t.shape[0]
    ids = lax.broadcasted_iota(jnp.int32, gate_t.shape, 0).astype(F32)
    g = jnp.where(ids < n_valid.astype(F32), gate_t, NEG_INF)
    sel = jnp.zeros(gate_t.shape, F32)
    for _ in range(MOBA_TOPK):
        mx = jnp.max(g, axis=0, keepdims=True)
        cand = jnp.where(g == mx, jnp.where(mx > NEG_INF, ids, float(nb)), float(nb))
        pick = ids == jnp.min(cand, axis=0, keepdims=True)
        sel = jnp.where(pick, 1.0, sel)
        g = jnp.where(pick, NEG_INF, g)
    return sel


def _moba_prompt_kernel(qt_ref, k_ref, vt_ref, kmean_ref, bd_ref, bp_ref, o_ref,
                        qs_ref, sel_ref, m_ref, l_ref, acc_ref, s_ref, p_ref):
    i = pl.program_id(1)
    for h in range(MOBA_HEADS):
        slab = slice((h // 2) * LANES, (h // 2 + 1) * LANES)
        qh = _masked_half(qt_ref[slab, :], h % 2)
        gate_t = jnp.dot(kmean_ref[:, slab].astype(BF16), qh, preferred_element_type=F32)
        sel_ref[h] = _top_blocks_t(gate_t, i)
        qs_ref[h] = qh * ATTN_SCALE

    def scores(h, j):
        rows = pl.ds(pl.multiple_of(j * T, T), T)
        slab = slice((h // 2) * LANES, (h // 2 + 1) * LANES)
        return jnp.dot(k_ref[rows, slab], qs_ref[h], preferred_element_type=F32)

    _stream_blocks(i, MOBA_HEADS, scores,
                   lambda h, j: vt_ref[j, h * HEAD_DIM:(h + 1) * HEAD_DIM, :],
                   lambda h, j: sel_ref[h, pl.ds(j, 1), :],
                   lambda h: bd_ref.at[h], lambda h: bp_ref.at[h], m_ref, l_ref, acc_ref, s_ref, p_ref, HEAD_DIM)
    for h in range(MOBA_HEADS):
        rows = slice(h * HEAD_DIM, (h + 1) * HEAD_DIM)
        o_ref[rows, :] = (acc_ref[rows, :] / l_ref[h]).astype(o_ref.dtype)


def _lambda(lam_ref, lam_init):
    lv = lam_ref[...]
    s1 = jnp.sum(lv[0:1] * lv[1:2], axis=-1, keepdims=True)
    s2 = jnp.sum(lv[2:3] * lv[3:4], axis=-1, keepdims=True)
    return jnp.exp(s1) - jnp.exp(s2) + lam_init


def _diff_prompt_kernel(qt_ref, k_ref, vt_ref, bd_ref, bp_ref, lam_ref, g_ref, o_ref,
                        qs_ref, m_ref, l_ref, acc_ref, s_ref, p_ref, *, lam_init):
    i = pl.program_id(1)
    for r in range(N_SUB):
        slab = slice((r // 2) * LANES, (r // 2 + 1) * LANES)
        qs_ref[r] = _masked_half(qt_ref[slab, :], r % 2) * ATTN_SCALE

    def scores(r, j):
        rows = pl.ds(pl.multiple_of(j * T, T), T)
        slab = slice((r // 2) * LANES, (r // 2 + 1) * LANES)
        return jnp.dot(k_ref[rows, slab], qs_ref[r], preferred_element_type=F32)

    _stream_blocks(i, N_SUB, scores,
                   lambda r, j: vt_ref[j, (r // 2) * DIFF_VDIM:(r // 2 + 1) * DIFF_VDIM, :],
                   lambda r, j: None,
                   lambda r: bd_ref.at[r // 2], lambda r: bp_ref.at[r // 2], m_ref, l_ref, acc_ref, s_ref, p_ref,
                   DIFF_VDIM)
    lam = _lambda(lam_ref, lam_init)
    for h in range(DIFF_HEADS):
        parts = []
        for c in range(2):
            r = 2 * h + c
            parts.append(acc_ref[r * DIFF_VDIM:(r + 1) * DIFF_VDIM, :] / l_ref[r])
        o = parts[0] - lam * parts[1]
        o = o * lax.rsqrt(jnp.mean(o * o, axis=0, keepdims=True) + LN_EPS) * g_ref[...] * (1.0 - lam_init)
        o_ref[h * DIFF_VDIM:(h + 1) * DIFF_VDIM, :] = o.astype(o_ref.dtype)


def _prompt_bias_tiles(tab):
    key = jnp.arange(T)[:, None]
    qry = jnp.arange(T)[None, :]
    return _bias_lookup(tab, qry - key), _bias_lookup(tab, qry - key + T)


def _moba_prompt(qt, kb, vt, kmean, tab_m, b, s):
    nq = s // T
    bd, bp = _prompt_bias_tiles(tab_m)
    const3 = lambda bi, i: (0, 0, 0)
    return pl.pallas_call(
        _moba_prompt_kernel,
        grid=(b, nq),
        in_specs=[pl.BlockSpec((None, MOBA_W, T), lambda bi, i: (bi * nq + i, 0, 0)),
                  pl.BlockSpec((s, MOBA_W), lambda bi, i: (bi, 0)),
                  pl.BlockSpec((nq, MOBA_W, T), lambda bi, i: (bi, 0, 0)),
                  pl.BlockSpec((None, nq, MOBA_W), lambda bi, i: (bi, 0, 0)),
                  pl.BlockSpec((MOBA_HEADS, T, T), const3),
                  pl.BlockSpec((MOBA_HEADS, T, T), const3)],
        out_specs=pl.BlockSpec((None, MOBA_W, T), lambda bi, i: (bi * nq + i, 0, 0)),
        out_shape=jax.ShapeDtypeStruct((b * nq, MOBA_W, T), BF16),
        scratch_shapes=[pltpu.VMEM((MOBA_HEADS, LANES, T), BF16), pltpu.VMEM((MOBA_HEADS, nq, T), F32),
                        pltpu.VMEM((MOBA_HEADS, 1, T), F32), pltpu.VMEM((MOBA_HEADS, 1, T), F32),
                        pltpu.VMEM((MOBA_W, T), F32),
                        pltpu.VMEM((MOBA_HEADS, T, T), F32), pltpu.VMEM((MOBA_HEADS, T, T), BF16)],
        compiler_params=_cparams(("parallel", "arbitrary")),
        name="moba_prompt",
    )(qt, kb, vt, kmean.reshape(b, nq, MOBA_W), bd, bp)


def _diff_prompt(qt, kb, vt, tab_d, lam_vecs, subln_g, lam_init, b, s):
    nq = s // T
    bd, bp = _prompt_bias_tiles(tab_d)
    g_cols = jnp.broadcast_to(subln_g.reshape(DIFF_VDIM, 1), (DIFF_VDIM, T))
    const3 = lambda bi, i: (0, 0, 0)
    const2 = lambda bi, i: (0, 0)
    return pl.pallas_call(
        functools.partial(_diff_prompt_kernel, lam_init=lam_init),
        grid=(b, nq),
        in_specs=[pl.BlockSpec((None, DIFF_W, T), lambda bi, i: (bi * nq + i, 1, 0)),
                  pl.BlockSpec((s, DIFF_W), lambda bi, i: (bi, 1)),
                  pl.BlockSpec((nq, DIFF_W, T), lambda bi, i: (bi, 1, 0)),
                  pl.BlockSpec((DIFF_HEADS, T, T), const3),
                  pl.BlockSpec((DIFF_HEADS, T, T), const3),
                  pl.BlockSpec((4, HEAD_DIM), const2),
                  pl.BlockSpec((DIFF_VDIM, T), const2)],
        out_specs=pl.BlockSpec((None, DIFF_W, T), lambda bi, i: (bi * nq + i, 0, 0)),
        out_shape=jax.ShapeDtypeStruct((b * nq, DIFF_W, T), BF16),
        scratch_shapes=[pltpu.VMEM((N_SUB, LANES, T), BF16),
                        pltpu.VMEM((N_SUB, 1, T), F32), pltpu.VMEM((N_SUB, 1, T), F32),
                        pltpu.VMEM((N_SUB * DIFF_VDIM, T), F32),
                        pltpu.VMEM((N_SUB, T, T), F32), pltpu.VMEM((N_SUB, T, T), BF16)],
        compiler_params=_cparams(("parallel", "arbitrary")),
        name="diff_prompt",
    )(qt, kb, vt, bd, bp, lam_vecs, g_cols)


def _sample_main_kernel(pt_ref, q_ref, knew_ref, vnew_ref, *refs, lam_init):
    del pt_ref
    k_refs = refs[:PAGES_PER_STEP]
    v_refs = refs[PAGES_PER_STEP:2 * PAGES_PER_STEP]
    (tail_ref, bias0_ref, lam_ref, g_ref, o_ref, sel_ref,
     qhi_ref, qlo_ref, gacc_ref, m_ref, l_ref, acc_ref) = refs[2 * PAGES_PER_STEP:]
    j = pl.program_id(1)
    nstep = pl.num_programs(1)
    ppb = MOBA_BLOCK // PAGE_SIZE

    @pl.when(j == 0)
    def _():
        row = lax.broadcasted_iota(jnp.int32, (LANES, DIFF_W), 0)
        sub_of_lane = lax.broadcasted_iota(jnp.int32, (LANES, DIFF_W), 1) // HEAD_DIM
        qm = jnp.where(row == sub_of_lane, jnp.broadcast_to(q_ref[:, MOBA_W:], (LANES, DIFF_W)), 0.0)
        qhi_ref[...], qlo_ref[...] = _split2((qm * ATTN_SCALE).T)
        m_ref[...] = jnp.full(m_ref.shape, M_INIT, F32)
        l_ref[...] = jnp.zeros(l_ref.shape, F32)
        acc_ref[...] = jnp.zeros(acc_ref.shape, F32)

    far = tail_ref[0:1, :]
    ksum = None
    page_m, page_l, page_acc = [], [], []
    for pg in range(PAGES_PER_STEP):
        k_ref, v_ref = k_refs[pg], v_refs[pg]
        part = jnp.sum(k_ref[:, :MOBA_W], axis=0, keepdims=True)
        ksum = part if pg % ppb == 0 else ksum + part
        if pg % ppb == ppb - 1:
            blk = j * (PAGES_PER_STEP // ppb) + pg // ppb
            gacc_ref[pl.ds(blk, 1), :] = ksum * (1.0 / MOBA_BLOCK) * q_ref[:, :MOBA_W]
        s = _dot3(*_split2(k_ref[:, MOBA_W:]), qhi_ref[...], qlo_ref[...])
        tail_pg = pg - (PAGES_PER_STEP - ppb)
        if tail_pg >= 0:
            bias = jnp.where(j == nstep - 1, tail_ref[tail_pg * PAGE_SIZE:(tail_pg + 1) * PAGE_SIZE, :], far)
        else:
            bias = far
        s = s + bias
        m_pg = _colmax(s)
        p = jnp.exp(s - m_pg)
        page_m.append(m_pg)
        page_l.append(_fold8(p))
        page_acc.append([_fold8(p[:, r:r + 1] * v_ref[:, (r // 2) * DIFF_VDIM:(r // 2 + 1) * DIFF_VDIM])
                         for r in range(N_SUB)])
    m_old = m_ref[...]
    m_new = m_old
    for m_pg in page_m:
        m_new = jnp.maximum(m_new, m_pg)
    alpha = jnp.exp(m_old - m_new)
    weights = [jnp.exp(m_pg - m_new) for m_pg in page_m]
    l_ref[...] = alpha * l_ref[...] + sum(w * l for w, l in zip(weights, page_l))
    m_ref[...] = m_new
    for r in range(N_SUB):
        acc_ref[r] = alpha[:, r:r + 1] * acc_ref[r] + sum(w[:, r:r + 1] * acc[r]
                                                          for w, acc in zip(weights, page_acc))

    @pl.when(j == nstep - 1)
    def _():
        kd = jnp.broadcast_to(knew_ref[:, MOBA_W:], (SUBLANES, DIFF_W))
        s_self = _dot3(*_split2(kd), qhi_ref[...], qlo_ref[...])[0:1] + bias0_ref[...]
        m_old = m_ref[...]
        m_new = jnp.maximum(m_old, s_self)
        alpha = jnp.exp(m_old - m_new)
        p_self = jnp.exp(s_self - m_new)
        l = alpha * jnp.sum(l_ref[...], axis=0, keepdims=True) + p_self
        lam = _lambda(lam_ref, lam_init)
        for h in range(DIFF_HEADS):
            vcols = slice(h * DIFF_VDIM, (h + 1) * DIFF_VDIM)
            parts = []
            for c in range(2):
                r = 2 * h + c
                num = (alpha[:, r:r + 1] * jnp.sum(acc_ref[r], axis=0, keepdims=True)
                       + p_self[:, r:r + 1] * vnew_ref[:, MOBA_W + h * DIFF_VDIM:MOBA_W + (h + 1) * DIFF_VDIM])
                parts.append(num / l[:, r:r + 1])
            oh = parts[0] - lam * parts[1]
            o_ref[:, vcols] = (oh * lax.rsqrt(jnp.mean(oh * oh, -1, keepdims=True) + LN_EPS)
                               * g_ref[...] * (1.0 - lam_init))
        head_row = lax.broadcasted_iota(jnp.int32, (MOBA_HEADS, MOBA_W), 0)
        head_of_lane = lax.broadcasted_iota(jnp.int32, (MOBA_HEADS, MOBA_W), 1) // HEAD_DIM
        seg = jnp.where(head_row == head_of_lane, 1.0, 0.0)
        gate = lax.dot_general(seg, gacc_ref[...], (((1,), (1,)), ((), ())), precision=HIGHEST,
                               preferred_element_type=F32)
        ids = lax.broadcasted_iota(jnp.int32, gate.shape, 1).astype(F32)
        out_lane = lax.broadcasted_iota(jnp.int32, (MOBA_HEADS, LANES), 1)
        picked = jnp.zeros((MOBA_HEADS, LANES), F32)
        for t in range(MOBA_TOPK):
            mx = jnp.max(gate, axis=-1, keepdims=True)
            idx = jnp.min(jnp.where(gate == mx, ids, float(gate.shape[-1])), axis=-1, keepdims=True)
            picked = jnp.where(out_lane == t, idx, picked)
            gate = jnp.where(ids == idx, NEG_INF, gate)
        sel_ref[...] = picked.astype(jnp.int32)


def _sample_main(page_table, q_s, k_s, v_s, cache_k, cache_v, layer, tab_d, lam_vecs, subln_g, lam_init):
    db, n_pages = page_table.shape
    past = n_pages * PAGE_SIZE
    assert n_pages % PAGES_PER_STEP == 0 and past // MOBA_BLOCK >= MOBA_TOPK
    nblk = past // MOBA_BLOCK
    nstep = n_pages // PAGES_PER_STEP
    tab_sub = jnp.repeat(tab_d, 2, axis=0)
    tail = _bias_lookup(tab_sub, MOBA_BLOCK - jnp.arange(MOBA_BLOCK)).T
    tail = jnp.pad(tail, ((0, 0), (0, LANES - N_SUB)))
    bias_self = jnp.pad(tab_sub[:, 0].reshape(1, N_SUB), ((0, 0), (0, LANES - N_SUB)))

    def page(pg, col):
        return lambda b, j, pt: (pt[b * n_pages + PAGES_PER_STEP * j + pg], layer, 0, col)

    tok = lambda b, j, pt: (b, 0, 0)
    const2 = lambda b, j, pt: (0, 0)
    grid_spec = pltpu.PrefetchScalarGridSpec(
        num_scalar_prefetch=1,
        grid=(db, nstep),
        in_specs=([pl.BlockSpec((None, 1, MIX_W), tok)] * 3
                  + [pl.BlockSpec((None, None, PAGE_SIZE, MIX_W), page(pg, 0)) for pg in range(PAGES_PER_STEP)]
                  + [pl.BlockSpec((None, None, PAGE_SIZE, DIFF_W), page(pg, 1)) for pg in range(PAGES_PER_STEP)]
                  + [pl.BlockSpec((MOBA_BLOCK, LANES), const2), pl.BlockSpec((1, LANES), const2),
                     pl.BlockSpec((4, HEAD_DIM), const2), pl.BlockSpec((1, DIFF_VDIM), const2)]),
        out_specs=[pl.BlockSpec((None, 1, DIFF_W), tok),
                   pl.BlockSpec((None, MOBA_HEADS, LANES), tok)],
        scratch_shapes=[pltpu.VMEM((DIFF_W, LANES), BF16), pltpu.VMEM((DIFF_W, LANES), BF16),
                        pltpu.VMEM((nblk, MOBA_W), F32),
                        pltpu.VMEM((1, LANES), F32), pltpu.VMEM((SUBLANES, LANES), F32),
                        pltpu.VMEM((N_SUB, SUBLANES, DIFF_VDIM), F32)],
    )
    o_d, sel = pl.pallas_call(
        functools.partial(_sample_main_kernel, lam_init=lam_init),
        grid_spec=grid_spec,
        out_shape=[jax.ShapeDtypeStruct((db, 1, DIFF_W), F32),
                   jax.ShapeDtypeStruct((db, MOBA_HEADS, LANES), jnp.int32)],
        compiler_params=_cparams(("parallel", "arbitrary")),
        name="sample_main",
    )(page_table.reshape(-1), q_s, k_s, v_s, *([cache_k] * PAGES_PER_STEP), *([cache_v] * PAGES_PER_STEP),
      tail, bias_self, lam_vecs, subln_g)
    return o_d.reshape(db, DIFF_W), sel[:, :, :MOBA_TOPK]


def _sample_moba_kernel(pt_ref, sel_ref, q_ref, knew_ref, vnew_ref, *refs):
    del pt_ref, sel_ref
    n_pg = MOBA_TOPK * (MOBA_BLOCK // PAGE_SIZE)
    k_refs = refs[:n_pg]
    v_refs = refs[n_pg:2 * n_pg]
    bias_refs = refs[2 * n_pg:2 * n_pg + MOBA_TOPK]
    bias0_ref, o_ref = refs[2 * n_pg + MOBA_TOPK:]
    h = pl.program_id(1)
    par = h % 2
    lane = lax.broadcasted_iota(jnp.int32, (1, LANES), 1)
    mine = (lane >= par * HEAD_DIM) & (lane < (par + 1) * HEAD_DIM)
    qm = jnp.where(mine, q_ref[...], 0.0) * ATTN_SCALE
    q_hi, q_lo = _split2(jnp.broadcast_to(qm, (LANES, LANES)).T)
    m = jnp.full((1, LANES), M_INIT, F32)
    l8 = jnp.zeros((SUBLANES, LANES), F32)
    acc = jnp.zeros((SUBLANES, LANES), F32)
    ppb = MOBA_BLOCK // PAGE_SIZE
    for pg in range(n_pg):
        brow = bias_refs[pg // ppb][:, (pg % ppb) * PAGE_SIZE:(pg % ppb + 1) * PAGE_SIZE]
        s = _dot3(*_split2(k_refs[pg][...]), q_hi, q_lo)
        s = s + jnp.broadcast_to(brow, (PAGE_SIZE, PAGE_SIZE)).T
        m_new = jnp.maximum(m, _colmax(s))
        p = jnp.exp(s - m_new)
        alpha = jnp.exp(m - m_new)
        l8 = alpha * l8 + _fold8(p)
        acc = alpha * acc + _fold8(p * v_refs[pg][...])
        m = m_new
    s_self = jnp.sum(qm * knew_ref[...], axis=-1, keepdims=True) + bias0_ref[:, 0:1]
    m_new = jnp.maximum(m, s_self)
    alpha = jnp.exp(m - m_new)
    p_self = jnp.exp(s_self - m_new)
    o = ((alpha * jnp.sum(acc, axis=0, keepdims=True) + p_self * vnew_ref[...])
         / (alpha * jnp.sum(l8, axis=0, keepdims=True) + p_self))
    o_ref[...] = jnp.where(par == 0, o[:, :HEAD_DIM], o[:, HEAD_DIM:])


def _sample_moba(page_table, sel, q_s, k_s, v_s, cache_k, cache_v, layer, tab_m):
    db, n_pages = page_table.shape
    past = n_pages * PAGE_SIZE
    ppb = MOBA_BLOCK // PAGE_SIZE
    n_pg = MOBA_TOPK * ppb
    bias_past = _bias_lookup(tab_m, past - jnp.arange(past)).reshape(MOBA_HEADS, 1, past)
    bias_self = jnp.broadcast_to(tab_m[:, 0:1, None], (MOBA_HEADS, 1, LANES))

    def sel_at(b, h, t, sl):
        return sl[(b * MOBA_HEADS + h) * MOBA_TOPK + t]

    def page(pg):
        return lambda b, h, pt, sl: (pt[b * n_pages + ppb * sel_at(b, h, pg // ppb, sl) + pg % ppb], layer, 0, h // 2)

    def bias_blk(t):
        return lambda b, h, pt, sl: (h, 0, sel_at(b, h, t, sl))

    tok = lambda b, h, pt, sl: (b, 0, h // 2)
    grid_spec = pltpu.PrefetchScalarGridSpec(
        num_scalar_prefetch=2,
        grid=(db, MOBA_HEADS),
        in_specs=([pl.BlockSpec((None, 1, LANES), tok)] * 3
                  + [pl.BlockSpec((None, None, PAGE_SIZE, LANES), page(pg)) for pg in range(n_pg)] * 2
                  + [pl.BlockSpec((None, 1, MOBA_BLOCK), bias_blk(t)) for t in range(MOBA_TOPK)]
                  + [pl.BlockSpec((None, 1, LANES), lambda b, h, pt, sl: (h, 0, 0))]),
        out_specs=pl.BlockSpec((None, None, 1, HEAD_DIM), lambda b, h, pt, sl: (b, h, 0, 0)),
    )
    o_m = pl.pallas_call(
        _sample_moba_kernel,
        grid_spec=grid_spec,
        out_shape=jax.ShapeDtypeStruct((db, MOBA_HEADS, 1, HEAD_DIM), F32),
        compiler_params=_cparams(("parallel", "arbitrary")),
        name="sample_moba",
    )(page_table.reshape(-1), sel.reshape(-1), q_s, k_s, v_s,
      *([cache_k] * n_pg), *([cache_v] * n_pg), *([bias_past] * MOBA_TOPK), bias_self)
    return o_m.reshape(db, MOBA_W)


def _post_attn_kernel(h_ref, mm_ref, md_ref, wo_ref, g_ref, b_ref, br_ref, *refs, alpha, precise, transposed):
    wr_refs, (a_ref, eid_ref, gate_ref) = refs[:-3], refs[-3:]
    if transposed:
        mm = mm_ref[...].astype(F32).T
        md = md_ref[...].astype(F32).T
    else:
        mm, md = mm_ref[...], md_ref[...]
    mix = _dot(mm, wo_ref[:MOBA_W, :], precise) + _dot(md, wo_ref[MOBA_W:, :], precise)
    a = _layer_norm(alpha * h_ref[...] + mix, g_ref[...], b_ref[...])
    _store_tiles(a_ref, a)
    if precise:
        logits = jnp.dot(a, wr_refs[0][...], precision=HIGHEST, preferred_element_type=F32)
    else:
        logits = _dot3(*_split2(a), wr_refs[0][...], wr_refs[1][...])
    logits = logits + br_ref[...]
    lane = lax.broadcasted_iota(jnp.int32, logits.shape, 1).astype(F32)
    none = float(LANES)
    gl = jnp.where(lane < N_GROUPS, logits, NEG_INF)
    gmax = jnp.max(gl, axis=-1, keepdims=True)
    g_gate = 1.0 / jnp.sum(jnp.exp(gl - gmax), axis=-1, keepdims=True)
    g_idx = jnp.min(jnp.where(gl == gmax, lane, none), axis=-1, keepdims=True)
    lo = N_GROUPS + EXPERTS_PER_GROUP * g_idx
    el = jnp.where(lane >= lo, jnp.where(lane < lo + EXPERTS_PER_GROUP, logits, NEG_INF), NEG_INF)
    m1 = jnp.max(el, axis=-1, keepdims=True)
    i1 = jnp.min(jnp.where(el == m1, lane, none), axis=-1, keepdims=True)
    el2 = jnp.where(lane == i1, NEG_INF, el)
    m2 = jnp.max(el2, axis=-1, keepdims=True)
    i2 = jnp.min(jnp.where(el2 == m2, lane, none), axis=-1, keepdims=True)
    r = jnp.exp(m2 - m1)
    w1 = g_gate / (1.0 + r)
    w2 = g_gate * r / (1.0 + r)
    eid_ref[...] = jnp.where(lane == 0.0, i1 - N_GROUPS, jnp.where(lane == 1.0, i2 - N_GROUPS, 0.0)).astype(jnp.int32)
    gate_ref[...] = jnp.where(lane == 0.0, w1, jnp.where(lane == 1.0, w2, 0.0))


def _post_attn(h, mix_m, mix_d, w_out, ln_g, ln_b, w_router, b_router, alpha, precise, tm, transposed):
    m = h.shape[0]
    row = lambda i: (i, 0)
    const = lambda i: (0, 0)
    if transposed:
        assert tm == T
        mix_specs = [pl.BlockSpec((None, MOBA_W, T), lambda i: (i, 0, 0)),
                     pl.BlockSpec((None, DIFF_W, T), lambda i: (i, 0, 0))]
    else:
        mix_specs = [pl.BlockSpec((tm, MOBA_W), row), pl.BlockSpec((tm, DIFF_W), row)]
    routers = (w_router,) if precise else _split2(w_router)
    return pl.pallas_call(
        functools.partial(_post_attn_kernel, alpha=alpha, precise=precise, transposed=transposed),
        grid=(m // tm,),
        in_specs=[pl.BlockSpec((tm, D_MODEL), row)] + mix_specs + [
            pl.BlockSpec((MIX_W, D_MODEL), const),
            pl.BlockSpec((1, D_MODEL), const), pl.BlockSpec((1, D_MODEL), const),
            pl.BlockSpec((1, LANES), const)] + [pl.BlockSpec((D_MODEL, LANES), const)] * len(routers),
        out_specs=[pl.BlockSpec((tm * N_CHUNK, LANES), row), pl.BlockSpec((tm, LANES), row),
                   pl.BlockSpec((tm, LANES), row)],
        out_shape=[jax.ShapeDtypeStruct((m * N_CHUNK, LANES), F32), jax.ShapeDtypeStruct((m, LANES), jnp.int32),
                   jax.ShapeDtypeStruct((m, LANES), F32)],
        compiler_params=_cparams(("parallel",)),
        name="post_attn",
    )(h, mix_m, mix_d, w_out, ln_g, ln_b, b_router, *routers)


def _moe_kernel(code_ref, blk_e_ref, blk_src_ref, blk_n_ref, nblk_ref, a_hbm, w1_ref, w3_ref, w2_ref, y_hbm,
                xbuf, ybuf, w1b, w3b, w2b, gsem, ssem, *, rb, n_tok):
    i = pl.program_id(0)
    n_used = nblk_ref[0]

    def tile(t):
        start = t * N_CHUNK
        return pl.ds(start if isinstance(t, int) else pl.multiple_of(start, N_CHUNK), N_CHUNK)

    def gather_row(base, r, slot):
        tok = code_ref[base + r] >> 1
        return pltpu.make_async_copy(a_hbm.at[tile(tok)], xbuf.at[slot, tile(r)], gsem.at[slot])

    def scatter_row(base, r, n_valid, slot):
        code = code_ref[base + r]
        dst = jnp.where(r < n_valid, (code >> 1) + (code & 1) * n_tok, 2 * n_tok + slot * rb + r)
        return pltpu.make_async_copy(ybuf.at[slot, tile(r)], y_hbm.at[tile(dst)], ssem.at[slot])

    def for_rows(fn):
        def body(r, carry):
            fn(r)
            return carry
        lax.fori_loop(0, rb, body, 0, unroll=SUBLANES)

    def wait_gather(slot):
        pltpu.make_async_copy(xbuf.at[slot], xbuf.at[slot], gsem.at[slot]).wait()

    def wait_scatter(slot):
        pltpu.make_async_copy(ybuf.at[slot], ybuf.at[slot], ssem.at[slot]).wait()

    @pl.when(i < n_used)
    def _():
        slot = i % 2
        other = 1 - slot

        @pl.when(i == 0)
        def _():
            base0 = blk_src_ref[0]
            for_rows(lambda r: gather_row(base0, r, 0).start())
            ybuf[1] = jnp.zeros(ybuf.shape[1:], F32)
            for half in range(2):
                for_rows(lambda r, half=half: pltpu.make_async_copy(
                    ybuf.at[1, tile(r)], y_hbm.at[tile(2 * n_tok + half * rb + r)], ssem.at[1]).start())
                wait_scatter(1)

        e = blk_e_ref[i]

        @pl.when((i == 0) | (e != blk_e_ref[jnp.maximum(i - 1, 0)]))
        def _():
            w1b[...] = w1_ref[...].astype(BF16)
            w3b[...] = w3_ref[...].astype(BF16)
            w2b[...] = w2_ref[...].astype(BF16)

        wait_gather(slot)

        @pl.when(i >= 1)
        def _():
            wait_scatter(slot)

        nxt_base = blk_src_ref[jnp.minimum(i + 1, pl.num_programs(0) - 1)]
        prv = jnp.maximum(i - 1, 0)
        prv_base = blk_src_ref[prv]
        prv_valid = jnp.where(i >= 1, blk_n_ref[prv], 0)

        def issue_rows(lo, hi):
            for r in range(lo, hi):
                gather_row(nxt_base, r, other).start()
                scatter_row(prv_base, r, prv_valid, other).start()

        cut1, cut2 = rb // 3, 2 * rb // 3
        x = _load_tiles(xbuf.at[slot]).astype(BF16)
        h1 = jnp.dot(x, w1b[...], preferred_element_type=F32)
        issue_rows(0, cut1)
        h3 = jnp.dot(x, w3b[...], preferred_element_type=F32)
        issue_rows(cut1, cut2)
        hidden = (h1 * jax.nn.sigmoid(h1) * h3).astype(BF16)
        y = jnp.dot(hidden, w2b[...], preferred_element_type=F32)
        issue_rows(cut2, rb)
        _store_tiles(ybuf.at[slot], y)

        @pl.when(i == n_used - 1)
        def _():
            base = blk_src_ref[i]
            n_valid = blk_n_ref[i]
            for_rows(lambda r: scatter_row(base, r, n_valid, slot).start())
            wait_scatter(slot)
            wait_scatter(other)
            wait_gather(other)


def _route_tables(eid, rb):
    n = eid.shape[0]
    m = n * 2
    e_flat = eid.reshape(m)
    keys = lax.sort(e_flat * m + jnp.arange(m, dtype=jnp.int32))
    code = keys % m
    experts = jnp.arange(N_EXPERTS, dtype=jnp.int32)
    counts = jnp.sum((e_flat[:, None] == experts[None, :]).astype(jnp.int32), axis=0)
    padded = (counts + rb - 1) // rb * rb
    start = jnp.cumsum(counts) - counts
    p_end = jnp.cumsum(padded)
    p_start = p_end - padded
    n_blk = -(-(m + N_EXPERTS * (rb - 1)) // rb)
    blk_row = jnp.arange(n_blk, dtype=jnp.int32) * rb
    blk_e = jnp.minimum(jnp.sum((p_end[None, :] <= blk_row[:, None]).astype(jnp.int32), axis=1), N_EXPERTS - 1)
    onehot = (blk_e[:, None] == experts[None, :]).astype(jnp.int32)
    pick = lambda table: jnp.sum(onehot * table[None, :], axis=1)
    off = blk_row - pick(p_start)
    blk_src = jnp.clip(pick(start) + off, 0, m - 1).astype(jnp.int32)
    blk_n = jnp.clip(pick(counts) - off, 0, rb).astype(jnp.int32)
    n_used = (p_end[-1:] // rb).astype(jnp.int32)
    code = jnp.pad(code.astype(jnp.int32), (0, rb))
    return code, blk_e.astype(jnp.int32), blk_src, blk_n, n_used, n_blk


def _moe(a_tiles, eid, w1, w3, w2, rb):
    n = a_tiles.shape[0] // N_CHUNK
    code, blk_e, blk_src, blk_n, n_used, n_blk = _route_tables(eid, rb)
    wmap = lambda i, code, be, bs, bn, nu: (be[i], 0, 0)
    grid_spec = pltpu.PrefetchScalarGridSpec(
        num_scalar_prefetch=5,
        grid=(n_blk,),
        in_specs=[pl.BlockSpec(memory_space=pl.ANY),
                  pl.BlockSpec((None, D_MODEL, D_EXPERT), wmap),
                  pl.BlockSpec((None, D_MODEL, D_EXPERT), wmap),
                  pl.BlockSpec((None, D_EXPERT, D_MODEL), wmap)],
        out_specs=pl.BlockSpec(memory_space=pl.ANY),
        scratch_shapes=[pltpu.VMEM((2, rb * N_CHUNK, LANES), F32), pltpu.VMEM((2, rb * N_CHUNK, LANES), F32),
                        pltpu.VMEM((D_MODEL, D_EXPERT), BF16), pltpu.VMEM((D_MODEL, D_EXPERT), BF16),
                        pltpu.VMEM((D_EXPERT, D_MODEL), BF16),
                        pltpu.SemaphoreType.DMA((2,)), pltpu.SemaphoreType.DMA((2,))],
    )
    return pl.pallas_call(
        functools.partial(_moe_kernel, rb=rb, n_tok=n),
        grid_spec=grid_spec,
        out_shape=jax.ShapeDtypeStruct(((2 * n + 2 * rb) * N_CHUNK, LANES), F32),
        compiler_params=_cparams(("arbitrary",)),
        name="moe",
    )(code, blk_e, blk_src, blk_n, n_used, a_tiles, w1, w3, w2)


def _finish_kernel(a_ref, y0_ref, y1_ref, gate_ref, p_ref, g_ref, b_ref, wpg_ref, bpg_ref, wple_ref, o_ref,
                   *, alpha, precise):
    gates = gate_ref[...]
    moe = _load_tiles(y0_ref) * gates[:, 0:1] + _load_tiles(y1_ref) * gates[:, 1:2]
    c = _layer_norm(alpha * _load_tiles(a_ref) + moe, g_ref[...], b_ref[...])
    gate = jax.nn.sigmoid(_dot(c, wpg_ref[...], precise) + bpg_ref[...])
    o_ref[...] = c + gate * _dot(p_ref[...], wple_ref[...], precise)


def _finish(a, y, gates, p, ln_g, ln_b, w_pg, b_pg, w_ple, alpha, precise, tm):
    m = a.shape[0] // N_CHUNK
    assert m % tm == 0
    row = lambda i: (i, 0)
    const = lambda i: (0, 0)
    tiles = (tm * N_CHUNK, LANES)
    return pl.pallas_call(
        functools.partial(_finish_kernel, alpha=alpha, precise=precise),
        grid=(m // tm,),
        in_specs=[pl.BlockSpec(tiles, row), pl.BlockSpec(tiles, row),
                  pl.BlockSpec(tiles, lambda i: (m // tm + i, 0)),
                  pl.BlockSpec((tm, LANES), row), pl.BlockSpec((tm, D_PLE), row),
                  pl.BlockSpec((1, D_MODEL), const), pl.BlockSpec((1, D_MODEL), const),
                  pl.BlockSpec((D_MODEL, D_MODEL), const), pl.BlockSpec((1, D_MODEL), const),
                  pl.BlockSpec((D_PLE, D_MODEL), const)],
        out_specs=pl.BlockSpec((tm, D_MODEL), row),
        out_shape=jax.ShapeDtypeStruct((m, D_MODEL), F32),
        compiler_params=_cparams(("parallel",)),
        name="finish",
    )(a, y, y, gates, p, ln_g, ln_b, w_pg, b_pg, w_ple)


def _pack_router(w_group, b_group, w_er, b_er):
    w = jnp.concatenate([w_group, jnp.transpose(w_er, (1, 0, 2)).reshape(D_MODEL, N_EXPERTS)], axis=1)
    b = jnp.concatenate([b_group, b_er.reshape(N_EXPERTS)])
    pad = LANES - w.shape[1]
    return jnp.pad(w, ((0, 0), (0, pad))), jnp.pad(b, (0, pad)).reshape(1, LANES)


def kernel(x_prompt, x_sample, cache_k, cache_v, page_table, p_prompt, p_sample, rel_bias, w_in, w_out, lam_q1, lam_k1, lam_q2, lam_k2, subln_g, ln1_g, ln1_b, w_group, b_group, w_erouter, b_erouter, w1, w3, w2, ln2_g, ln2_b, w_ple, w_pg, b_pg):
    depth = w_in.shape[0]
    b, s, _ = x_prompt.shape
    db, ds, _ = x_sample.shape
    assert ds == 1 and s % T == 0
    alpha = (2 * depth) ** 0.25
    tab_m = rel_bias[:, :MOBA_HEADS].T
    tab_d = rel_bias[:, MOBA_HEADS:].T
    hp = x_prompt.reshape(b * s, D_MODEL)
    hs = x_sample.reshape(db, D_MODEL)
    kp_rows, vp_rows, ks_rows, vs_rows = [], [], [], []
    for i in range(depth):
        lam_init = 0.8 - 0.6 * math.exp(-0.3 * i)
        lam_vecs = jnp.stack([lam_q1[i], lam_k1[i], lam_q2[i], lam_k2[i]]).astype(F32)
        g_sub = subln_g[i].reshape(1, DIFF_VDIM)
        w_router, b_router = _pack_router(w_group[i], b_group[i], w_erouter[i], b_erouter[i])
        ln1 = (ln1_g[i].reshape(1, D_MODEL), ln1_b[i].reshape(1, D_MODEL))
        ln2 = (ln2_g[i].reshape(1, D_MODEL), ln2_b[i].reshape(1, D_MODEL))
        b_pg_i = b_pg[i].reshape(1, D_MODEL)

        k_p, v_p, qt, kb, vt, kmean = _qkv_prompt(hp, w_in[i].astype(BF16))
        mix_m = _moba_prompt(qt, kb, vt, kmean, tab_m, b, s)
        mix_d = _diff_prompt(qt, kb, vt, tab_d, lam_vecs, g_sub, lam_init, b, s)
        a_p, eid_p, gate_p = _post_attn(hp, mix_m, mix_d, w_out[i].astype(BF16), *ln1, w_router, b_router,
                                        alpha, False, T, True)
        y_p = _moe(a_p, eid_p[:, :2], w1[i], w3[i], w2[i], 256)
        hp = _finish(a_p, y_p, gate_p, p_prompt[i].reshape(b * s, D_PLE), *ln2, w_pg[i].astype(BF16), b_pg_i,
                     w_ple[i].astype(BF16), alpha, False, 256)

        qkv_s = _mm_precise(hs, w_in[i], MIX_W)
        q_s = qkv_s[:, :MIX_W].reshape(db, 1, MIX_W)
        k_s = qkv_s[:, MIX_W:2 * MIX_W].reshape(db, 1, MIX_W)
        v_s = qkv_s[:, 2 * MIX_W:].reshape(db, 1, MIX_W)
        o_d, sel = _sample_main(page_table, q_s, k_s, v_s, cache_k, cache_v, i, tab_d, lam_vecs, g_sub, lam_init)
        o_m = _sample_moba(page_table, sel, q_s, k_s, v_s, cache_k, cache_v, i, tab_m)
        a_s, eid_s, gate_s = _post_attn(hs, o_m, o_d, w_out[i], *ln1, w_router, b_router, alpha, True, db, False)
        y_s = _moe(a_s, eid_s[:, :2], w1[i], w3[i], w2[i], 64)
        hs = _finish(a_s, y_s, gate_s, p_sample[i].reshape(db, D_PLE), *ln2, w_pg[i], b_pg_i, w_ple[i],
                     alpha, True, db)

        kp_rows.append(k_p.reshape(b, s, MIX_W))
        vp_rows.append(v_p.reshape(b, s, MIX_W))
        ks_rows.append(k_s.reshape(db, ds, MIX_W))
        vs_rows.append(v_s.reshape(db, ds, MIX_W))
    return (hp.reshape(b, s, D_MODEL), hs.reshape(db, ds, D_MODEL),
            jnp.stack(kp_rows, axis=1), jnp.stack(vp_rows, axis=1),
            jnp.stack(ks_rows, axis=1), jnp.stack(vs_rows, axis=1))
```

```python
import functools
import math

import jax
import jax.numpy as jnp
from jax import lax
from jax.experimental import pallas as pl
from jax.experimental.pallas import tpu as pltpu

D_MODEL = 1024
PAGE_SIZE = 128
HEAD_DIM = 64
MOBA_HEADS = 8
MOBA_W = MOBA_HEADS * HEAD_DIM
DIFF_HEADS = 4
DIFF_VDIM = 2 * HEAD_DIM
DIFF_W = DIFF_HEADS * DIFF_VDIM
N_SUB = 2 * DIFF_HEADS
MIX_W = MOBA_W + DIFF_W
MOBA_BLOCK = 256
MOBA_TOPK = 3
N_BUCKETS = 32
REL_MAX_DIST = 128
N_GROUPS = 4
EXPERTS_PER_GROUP = 8
N_EXPERTS = N_GROUPS * EXPERTS_PER_GROUP
D_EXPERT = D_MODEL // 2
D_PLE = 256
LN_EPS = 1e-5
ATTN_SCALE = HEAD_DIM ** -0.5
LANES = 128
SUBLANES = 8
N_CHUNK = D_MODEL // LANES
PAGES_PER_STEP = 8
F32 = jnp.float32
BF16 = jnp.bfloat16
HIGHEST = lax.Precision.HIGHEST
NEG_INF = float("-inf")
M_INIT = -1e30
VMEM_LIMIT = 48 * 1024 * 1024
T = MOBA_BLOCK


def _cparams(sem):
    return pltpu.CompilerParams(dimension_semantics=sem, vmem_limit_bytes=VMEM_LIMIT)


def _dot(a, b):
    return jnp.dot(a.astype(BF16), b.astype(BF16), preferred_element_type=F32)


def _mxu_round(x):
    return x.astype(BF16).astype(F32)


def _rel_bucket(rel):
    n = jnp.maximum(rel, 0)
    max_exact = N_BUCKETS // 2
    large = max_exact + (jnp.log(jnp.maximum(n, 1).astype(F32) / max_exact)
                         / math.log(REL_MAX_DIST / max_exact) * (N_BUCKETS - max_exact)).astype(jnp.int32)
    large = jnp.minimum(large, N_BUCKETS - 1)
    return jnp.where(n < max_exact, n, large)


def _bias_lookup(tab, rel):
    onehot = (_rel_bucket(rel)[..., None] == jnp.arange(N_BUCKETS)).astype(F32)
    return jnp.einsum('hb,...b->h...', tab, onehot, precision=HIGHEST)


def _layer_norm(x, g, b):
    xc = x - jnp.mean(x, -1, keepdims=True)
    var = jnp.mean(xc * xc, -1, keepdims=True)
    return xc * lax.rsqrt(var + LN_EPS) * g + b


def _fold8(x):
    return jnp.sum(x.reshape(x.shape[0] // SUBLANES, SUBLANES, x.shape[1]), axis=0)


def _store_tiles(ref, x):
    rows = x.shape[0]
    for j in range(N_CHUNK):
        ref[pl.ds(j, rows, stride=N_CHUNK), :] = x[:, j * LANES:(j + 1) * LANES]


def _load_tiles(ref):
    rows = ref.shape[0] // N_CHUNK
    return jnp.concatenate([ref[pl.ds(j, rows, stride=N_CHUNK), :] for j in range(N_CHUNK)], axis=1)


def _colmax(x):
    part = jnp.max(x.reshape(x.shape[0] // SUBLANES, SUBLANES, x.shape[1]), axis=0)
    return jnp.max(part, axis=0, keepdims=True)


def _qkv_prompt_kernel(x_ref, w_ref, k_ref, v_ref, qt_ref, kb_ref, vt_ref, kmean_ref):
    x = x_ref[...].astype(BF16)
    q = jnp.dot(x, w_ref[:, :MIX_W], preferred_element_type=F32)
    qt_ref[...] = q.T.astype(BF16)
    k = jnp.dot(x, w_ref[:, MIX_W:2 * MIX_W], preferred_element_type=F32)
    k_ref[...] = k
    kb_ref[...] = k.astype(BF16)
    kmean_ref[...] = jnp.sum(k[:, :MOBA_W], axis=0, keepdims=True) * (1.0 / MOBA_BLOCK)
    v = jnp.dot(x, w_ref[:, 2 * MIX_W:], preferred_element_type=F32)
    v_ref[...] = v
    vt_ref[...] = v.T.astype(BF16)


def _qkv_prompt(x, w_bf16):
    m = x.shape[0]
    nt = m // T
    row = lambda i: (i, 0)
    tile3 = lambda i: (i, 0, 0)
    return pl.pallas_call(
        _qkv_prompt_kernel,
        grid=(nt,),
        in_specs=[pl.BlockSpec((T, D_MODEL), row),
                  pl.BlockSpec((D_MODEL, 3 * MIX_W), lambda i: (0, 0))],
        out_specs=[pl.BlockSpec((T, MIX_W), row), pl.BlockSpec((T, MIX_W), row),
                   pl.BlockSpec((None, MIX_W, T), tile3), pl.BlockSpec((T, MIX_W), row),
                   pl.BlockSpec((None, MIX_W, T), tile3),
                   pl.BlockSpec((None, 1, MOBA_W), tile3)],
        out_shape=[jax.ShapeDtypeStruct((m, MIX_W), F32), jax.ShapeDtypeStruct((m, MIX_W), F32),
                   jax.ShapeDtypeStruct((nt, MIX_W, T), BF16), jax.ShapeDtypeStruct((m, MIX_W), BF16),
                   jax.ShapeDtypeStruct((nt, MIX_W, T), BF16),
                   jax.ShapeDtypeStruct((nt, 1, MOBA_W), F32)],
        compiler_params=_cparams(("parallel",)),
        name="qkv_prompt",
    )(x, w_bf16)


def _mm_small_kernel(x_ref, w_ref, o_ref):
    o_ref[...] = _dot(x_ref[...], w_ref[...])


def _mm_small(x, w, tn):
    m, k = x.shape
    n = w.shape[1]
    return pl.pallas_call(
        _mm_small_kernel,
        grid=(n // tn,),
        in_specs=[pl.BlockSpec((m, k), lambda j: (0, 0)), pl.BlockSpec((k, tn), lambda j: (0, j))],
        out_specs=pl.BlockSpec((m, tn), lambda j: (0, j)),
        out_shape=jax.ShapeDtypeStruct((m, n), F32),
        compiler_params=_cparams(("parallel",)),
        name="mm_small",
    )(x, w)


def _stream_blocks(i, n_streams, scores, values, keep_row, bias_d, bias_p,
                   m_ref, l_ref, acc_ref, s_ref, p_ref, a_ref, dv):
    m_ref[...] = jnp.full(m_ref.shape, M_INIT, F32)
    l_ref[...] = jnp.zeros(l_ref.shape, F32)
    acc_ref[...] = jnp.zeros(acc_ref.shape, F32)

    def block(j, finish_scores, shift_of, gated):
        col_max = []
        for r in range(n_streams):
            s = finish_scores(r, j, scores(r, j))
            s_ref[0, r] = s
            col_max.append(_colmax(s))
        alphas = []
        for r in range(n_streams):
            shift = shift_of(r)
            m_old = m_ref[r]
            tile_max = col_max[r] + shift
            keep = keep_row(r, j) if gated else None
            if keep is not None:
                tile_max = jnp.where(keep > 0.5, tile_max, NEG_INF)
            m_new = jnp.maximum(m_old, tile_max)
            offset = m_new - shift
            if keep is not None:
                offset = jnp.where(keep > 0.5, offset, float("inf"))
            p = jnp.exp(s_ref[0, r] - offset)
            alpha = jnp.exp(m_old - m_new)
            p_ref[0, r] = p.astype(p_ref.dtype)
            m_ref[r] = m_new
            alphas.append(alpha)
        ones = jnp.ones((2 * SUBLANES, T), p_ref.dtype)
        for r in range(n_streams):
            rows = slice(r * dv, (r + 1) * dv)
            out = jnp.dot(jnp.concatenate([values(r, j), ones], axis=0), p_ref[0, r], preferred_element_type=F32)
            acc_ref[rows, :] = alphas[r] * acc_ref[rows, :] + out[:dv]
            l_ref[r] = alphas[r] * l_ref[r] + out[dv:dv + 1]

    n_far = jnp.maximum(i - 1, 0)
    last_far = jnp.maximum(n_far - 1, 0)

    def qk(j, slot):
        for r in range(n_streams):
            s_ref[slot, r] = scores(r, j)

    def softmax(t, slot):
        live = jnp.where(t < n_far, 0.0, NEG_INF).astype(F32).reshape(1, 1)
        for r in range(n_streams):
            shift = bias_p(r)[0:1, 0:1] + live
            s = s_ref[slot, r]
            m_old = m_ref[r]
            tile_max = _colmax(s) + shift
            keep = keep_row(r, t)
            if keep is not None:
                tile_max = jnp.where(keep > 0.5, tile_max, NEG_INF)
            m_new = jnp.maximum(m_old, tile_max)
            offset = m_new - shift
            if keep is not None:
                offset = jnp.where(keep > 0.5, offset, float("inf"))
            p_ref[slot, r] = jnp.exp(s - offset).astype(p_ref.dtype)
            a_ref[slot, r] = jnp.exp(m_old - m_new)
            m_ref[r] = m_new

    def pv(j, slot):
        ones = jnp.ones((2 * SUBLANES, T), p_ref.dtype)
        for r in range(n_streams):
            rows = slice(r * dv, (r + 1) * dv)
            out = jnp.dot(jnp.concatenate([values(r, j), ones], axis=0), p_ref[slot, r],
                          preferred_element_type=F32)
            alpha = a_ref[slot, r]
            acc_ref[rows, :] = alpha * acc_ref[rows, :] + out[:dv]
            l_ref[r] = alpha * l_ref[r] + out[dv:dv + 1]

    p_ref[1] = jnp.zeros(p_ref.shape[1:], p_ref.dtype)
    a_ref[1] = jnp.ones(a_ref.shape[1:], F32)
    qk(0, 0)

    def far_pair(u, carry):
        for slot in range(2):
            t = 2 * u + slot
            qk(jnp.minimum(t + 1, last_far), 1 - slot)
            softmax(t, slot)
            pv(jnp.maximum(t - 1, 0), 1 - slot)
        return carry

    n_pairs = (n_far + 1) // 2
    lax.fori_loop(0, n_pairs, far_pair, 0)
    pv(jnp.maximum(2 * n_pairs - 1, 0), 1)

    prev_shift = jnp.where(i >= 1, 0.0, NEG_INF).astype(F32).reshape(1, 1)
    block(jnp.maximum(i - 1, 0), lambda r, j, s: s + bias_p(r)[...], lambda r: prev_shift, True)

    key = lax.broadcasted_iota(jnp.int32, (T, T), 0)
    qry = lax.broadcasted_iota(jnp.int32, (T, T), 1)
    zero = jnp.zeros((1, 1), F32)
    block(i, lambda r, j, s: jnp.where(qry >= key, s + bias_d(r)[...], NEG_INF), lambda r: zero, False)


def _masked_half(qt_slab, half):
    row = lax.broadcasted_iota(jnp.int32, qt_slab.shape, 0)
    mine = (row >= half * HEAD_DIM) & (row < (half + 1) * HEAD_DIM)
    return jnp.where(mine, qt_slab, jnp.zeros_like(qt_slab))


def _top_blocks_t(gate_t, n_valid):
    nb = gate_t.shape[0]
    ids = lax.broadcasted_iota(jnp.int32, gate_t.shape, 0).astype(F32)
    g = jnp.where(ids < n_valid.astype(F32), gate_t, NEG_INF)
    sel = jnp.zeros(gate_t.shape, F32)
    for _ in range(MOBA_TOPK):
        mx = jnp.max(g, axis=0, keepdims=True)
        cand = jnp.where(g == mx, jnp.where(mx > NEG_INF, ids, float(nb)), float(nb))
        pick = ids == jnp.min(cand, axis=0, keepdims=True)
        sel = jnp.where(pick, 1.0, sel)
        g = jnp.where(pick, NEG_INF, g)
    return sel


def _moba_prompt_kernel(qt_ref, k_ref, vt_ref, kmean_ref, bd_ref, bp_ref, o_ref,
                        qs_ref, sel_ref, m_ref, l_ref, acc_ref, s_ref, p_ref, a_ref):
    i = pl.program_id(1)
    for h in range(MOBA_HEADS):
        slab = slice((h // 2) * LANES, (h // 2 + 1) * LANES)
        qh = _masked_half(qt_ref[slab, :], h % 2)
        gate_t = jnp.dot(kmean_ref[:, slab].astype(BF16), qh, preferred_element_type=F32)
        sel_ref[h] = _top_blocks_t(gate_t, i)
        qs_ref[h] = qh * ATTN_SCALE

    def scores(h, j):
        rows = pl.ds(pl.multiple_of(j * T, T), T)
        slab = slice((h // 2) * LANES, (h // 2 + 1) * LANES)
        return jnp.dot(k_ref[rows, slab], qs_ref[h], preferred_element_type=F32)

    _stream_blocks(i, MOBA_HEADS, scores,
                   lambda h, j: vt_ref[j, h * HEAD_DIM:(h + 1) * HEAD_DIM, :],
                   lambda h, j: sel_ref[h, pl.ds(j, 1), :],
                   lambda h: bd_ref.at[h], lambda h: bp_ref.at[h], m_ref, l_ref, acc_ref, s_ref, p_ref, a_ref,
                   HEAD_DIM)
    for h in range(MOBA_HEADS):
        rows = slice(h * HEAD_DIM, (h + 1) * HEAD_DIM)
        o_ref[rows, :] = (acc_ref[rows, :] / l_ref[h]).astype(o_ref.dtype)


def _lambda(lam_ref, lam_init):
    lv = lam_ref[...]
    s1 = jnp.sum(lv[0:1] * lv[1:2], axis=-1, keepdims=True)
    s2 = jnp.sum(lv[2:3] * lv[3:4], axis=-1, keepdims=True)
    return jnp.exp(s1) - jnp.exp(s2) + lam_init


def _diff_prompt_kernel(qt_ref, k_ref, vt_ref, bd_ref, bp_ref, lam_ref, g_ref, o_ref,
                        qs_ref, m_ref, l_ref, acc_ref, s_ref, p_ref, a_ref, *, lam_init):
    i = pl.program_id(1)
    for r in range(N_SUB):
        slab = slice((r // 2) * LANES, (r // 2 + 1) * LANES)
        qs_ref[r] = _masked_half(qt_ref[slab, :], r % 2) * ATTN_SCALE

    def scores(r, j):
        rows = pl.ds(pl.multiple_of(j * T, T), T)
        slab = slice((r // 2) * LANES, (r // 2 + 1) * LANES)
        return jnp.dot(k_ref[rows, slab], qs_ref[r], preferred_element_type=F32)

    _stream_blocks(i, N_SUB, scores,
                   lambda r, j: vt_ref[j, (r // 2) * DIFF_VDIM:(r // 2 + 1) * DIFF_VDIM, :],
                   lambda r, j: None,
                   lambda r: bd_ref.at[r // 2], lambda r: bp_ref.at[r // 2], m_ref, l_ref, acc_ref, s_ref, p_ref, a_ref,
                   DIFF_VDIM)
    lam = _lambda(lam_ref, lam_init)
    for h in range(DIFF_HEADS):
        parts = []
        for c in range(2):
            r = 2 * h + c
            parts.append(acc_ref[r * DIFF_VDIM:(r + 1) * DIFF_VDIM, :] / l_ref[r])
        o = parts[0] - lam * parts[1]
        o = o * lax.rsqrt(jnp.mean(o * o, axis=0, keepdims=True) + LN_EPS) * g_ref[...] * (1.0 - lam_init)
        o_ref[h * DIFF_VDIM:(h + 1) * DIFF_VDIM, :] = o.astype(o_ref.dtype)


def _prompt_bias_tiles(tab):
    key = jnp.arange(T)[:, None]
    qry = jnp.arange(T)[None, :]
    return _bias_lookup(tab, qry - key), _bias_lookup(tab, qry - key + T)


def _moba_prompt(qt, kb, vt, kmean, tab_m, b, s):
    nq = s // T
    bd, bp = _prompt_bias_tiles(tab_m)
    const3 = lambda bi, i: (0, 0, 0)
    return pl.pallas_call(
        _moba_prompt_kernel,
        grid=(b, nq),
        in_specs=[pl.BlockSpec((None, MOBA_W, T), lambda bi, i: (bi * nq + i, 0, 0)),
                  pl.BlockSpec((s, MOBA_W), lambda bi, i: (bi, 0)),
                  pl.BlockSpec((nq, MOBA_W, T), lambda bi, i: (bi, 0, 0)),
                  pl.BlockSpec((None, nq, MOBA_W), lambda bi, i: (bi, 0, 0)),
                  pl.BlockSpec((MOBA_HEADS, T, T), const3),
                  pl.BlockSpec((MOBA_HEADS, T, T), const3)],
        out_specs=pl.BlockSpec((None, MOBA_W, T), lambda bi, i: (bi * nq + i, 0, 0)),
        out_shape=jax.ShapeDtypeStruct((b * nq, MOBA_W, T), BF16),
        scratch_shapes=[pltpu.VMEM((MOBA_HEADS, LANES, T), BF16), pltpu.VMEM((MOBA_HEADS, nq, T), F32),
                        pltpu.VMEM((MOBA_HEADS, 1, T), F32), pltpu.VMEM((MOBA_HEADS, 1, T), F32),
                        pltpu.VMEM((MOBA_W, T), F32),
                        pltpu.VMEM((2, MOBA_HEADS, T, T), F32), pltpu.VMEM((2, MOBA_HEADS, T, T), BF16),
                        pltpu.VMEM((2, MOBA_HEADS, 1, T), F32)],
        compiler_params=_cparams(("parallel", "arbitrary")),
        name="moba_prompt",
    )(qt, kb, vt, kmean.reshape(b, nq, MOBA_W), bd, bp)


def _diff_prompt(qt, kb, vt, tab_d, lam_vecs, subln_g, lam_init, b, s):
    nq = s // T
    bd, bp = _prompt_bias_tiles(tab_d)
    g_cols = jnp.broadcast_to(subln_g.reshape(DIFF_VDIM, 1), (DIFF_VDIM, T))
    const3 = lambda bi, i: (0, 0, 0)
    const2 = lambda bi, i: (0, 0)
    return pl.pallas_call(
        functools.partial(_diff_prompt_kernel, lam_init=lam_init),
        grid=(b, nq),
        in_specs=[pl.BlockSpec((None, DIFF_W, T), lambda bi, i: (bi * nq + i, 1, 0)),
                  pl.BlockSpec((s, DIFF_W), lambda bi, i: (bi, 1)),
                  pl.BlockSpec((nq, DIFF_W, T), lambda bi, i: (bi, 1, 0)),
                  pl.BlockSpec((DIFF_HEADS, T, T), const3),
                  pl.BlockSpec((DIFF_HEADS, T, T), const3),
                  pl.BlockSpec((4, HEAD_DIM), const2),
                  pl.BlockSpec((DIFF_VDIM, T), const2)],
        out_specs=pl.BlockSpec((None, DIFF_W, T), lambda bi, i: (bi * nq + i, 0, 0)),
        out_shape=jax.ShapeDtypeStruct((b * nq, DIFF_W, T), BF16),
        scratch_shapes=[pltpu.VMEM((N_SUB, LANES, T), BF16),
                        pltpu.VMEM((N_SUB, 1, T), F32), pltpu.VMEM((N_SUB, 1, T), F32),
                        pltpu.VMEM((N_SUB * DIFF_VDIM, T), F32),
                        pltpu.VMEM((2, N_SUB, T, T), F32), pltpu.VMEM((2, N_SUB, T, T), BF16),
                        pltpu.VMEM((2, N_SUB, 1, T), F32)],
        compiler_params=_cparams(("parallel", "arbitrary")),
        name="diff_prompt",
    )(qt, kb, vt, bd, bp, lam_vecs, g_cols)


def _sample_main_kernel(pt_ref, q_ref, knew_ref, vnew_ref, *refs, lam_init):
    del pt_ref
    k_refs = refs[:PAGES_PER_STEP]
    v_refs = refs[PAGES_PER_STEP:2 * PAGES_PER_STEP]
    (tail_ref, bias0_ref, lam_ref, g_ref, o_ref, sel_ref,
     qb_ref, gacc_ref, m_ref, l_ref, acc_ref) = refs[2 * PAGES_PER_STEP:]
    j = pl.program_id(1)
    nstep = pl.num_programs(1)
    ppb = MOBA_BLOCK // PAGE_SIZE

    @pl.when(j == 0)
    def _():
        row = lax.broadcasted_iota(jnp.int32, (LANES, DIFF_W), 0)
        sub_of_lane = lax.broadcasted_iota(jnp.int32, (LANES, DIFF_W), 1) // HEAD_DIM
        qm = jnp.where(row == sub_of_lane, jnp.broadcast_to(q_ref[:, MOBA_W:], (LANES, DIFF_W)), 0.0)
        qb_ref[...] = (qm * ATTN_SCALE).T.astype(BF16)
        m_ref[...] = jnp.full(m_ref.shape, M_INIT, F32)
        l_ref[...] = jnp.zeros(l_ref.shape, F32)
        acc_ref[...] = jnp.zeros(acc_ref.shape, F32)

    far = tail_ref[0:1, :]
    ksum = None
    page_m, page_l, page_acc = [], [], []
    for pg in range(PAGES_PER_STEP):
        k_ref, v_ref = k_refs[pg], v_refs[pg]
        part = jnp.sum(k_ref[:, :MOBA_W], axis=0, keepdims=True)
        ksum = part if pg % ppb == 0 else ksum + part
        if pg % ppb == ppb - 1:
            blk = j * (PAGES_PER_STEP // ppb) + pg // ppb
            gacc_ref[pl.ds(blk, 1), :] = _mxu_round(ksum * (1.0 / MOBA_BLOCK)) * _mxu_round(q_ref[:, :MOBA_W])
        s = _dot(k_ref[:, MOBA_W:], qb_ref[...])
        tail_pg = pg - (PAGES_PER_STEP - ppb)
        if tail_pg >= 0:
            bias = jnp.where(j == nstep - 1, tail_ref[tail_pg * PAGE_SIZE:(tail_pg + 1) * PAGE_SIZE, :], far)
        else:
            bias = far
        s = s + bias
        m_pg = _colmax(s)
        p = jnp.exp(s - m_pg)
        page_m.append(m_pg)
        page_l.append(_fold8(p))
        pr = _mxu_round(p)
        vr = _mxu_round(v_ref[...])
        page_acc.append([_fold8(pr[:, r:r + 1] * vr[:, (r // 2) * DIFF_VDIM:(r // 2 + 1) * DIFF_VDIM])
                         for r in range(N_SUB)])
    m_old = m_ref[...]
    m_new = m_old
    for m_pg in page_m:
        m_new = jnp.maximum(m_new, m_pg)
    alpha = jnp.exp(m_old - m_new)
    weights = [jnp.exp(m_pg - m_new) for m_pg in page_m]
    l_ref[...] = alpha * l_ref[...] + sum(w * l for w, l in zip(weights, page_l))
    m_ref[...] = m_new
    for r in range(N_SUB):
        acc_ref[r] = alpha[:, r:r + 1] * acc_ref[r] + sum(w[:, r:r + 1] * acc[r]
                                                          for w, acc in zip(weights, page_acc))

    @pl.when(j == nstep - 1)
    def _():
        kd = jnp.broadcast_to(knew_ref[:, MOBA_W:], (SUBLANES, DIFF_W))
        s_self = _dot(kd, qb_ref[...])[0:1] + bias0_ref[...]
        m_old = m_ref[...]
        m_new = jnp.maximum(m_old, s_self)
        alpha = jnp.exp(m_old - m_new)
        p_self = jnp.exp(s_self - m_new)
        l = alpha * jnp.sum(l_ref[...], axis=0, keepdims=True) + p_self
        pr_self = _mxu_round(p_self)
        v_self = _mxu_round(vnew_ref[:, MOBA_W:])
        lam = _lambda(lam_ref, lam_init)
        for h in range(DIFF_HEADS):
            vcols = slice(h * DIFF_VDIM, (h + 1) * DIFF_VDIM)
            parts = []
            for c in range(2):
                r = 2 * h + c
                num = (alpha[:, r:r + 1] * jnp.sum(acc_ref[r], axis=0, keepdims=True)
                       + pr_self[:, r:r + 1] * v_self[:, vcols])
                parts.append(num / l[:, r:r + 1])
            oh = parts[0] - lam * parts[1]
            o_ref[:, vcols] = (oh * lax.rsqrt(jnp.mean(oh * oh, -1, keepdims=True) + LN_EPS)
                               * g_ref[...] * (1.0 - lam_init))
        head_row = lax.broadcasted_iota(jnp.int32, (MOBA_HEADS, MOBA_W), 0)
        head_of_lane = lax.broadcasted_iota(jnp.int32, (MOBA_HEADS, MOBA_W), 1) // HEAD_DIM
        seg = jnp.where(head_row == head_of_lane, 1.0, 0.0)
        gate = lax.dot_general(seg, gacc_ref[...], (((1,), (1,)), ((), ())), precision=HIGHEST,
                               preferred_element_type=F32)
        ids = lax.broadcasted_iota(jnp.int32, gate.shape, 1).astype(F32)
        out_lane = lax.broadcasted_iota(jnp.int32, (MOBA_HEADS, LANES), 1)
        picked = jnp.zeros((MOBA_HEADS, LANES), F32)
        for t in range(MOBA_TOPK):
            mx = jnp.max(gate, axis=-1, keepdims=True)
            idx = jnp.min(jnp.where(gate == mx, ids, float(gate.shape[-1])), axis=-1, keepdims=True)
            picked = jnp.where(out_lane == t, idx, picked)
            gate = jnp.where(ids == idx, NEG_INF, gate)
        sel_ref[...] = picked.astype(jnp.int32)


def _sample_main(page_table, q_s, k_s, v_s, cache_k, cache_v, layer, tab_d, lam_vecs, subln_g, lam_init):
    db, n_pages = page_table.shape
    past = n_pages * PAGE_SIZE
    assert n_pages % PAGES_PER_STEP == 0 and past // MOBA_BLOCK >= MOBA_TOPK
    nblk = past // MOBA_BLOCK
    nstep = n_pages // PAGES_PER_STEP
    tab_sub = jnp.repeat(tab_d, 2, axis=0)
    tail = _bias_lookup(tab_sub, MOBA_BLOCK - jnp.arange(MOBA_BLOCK)).T
    tail = jnp.pad(tail, ((0, 0), (0, LANES - N_SUB)))
    bias_self = jnp.pad(tab_sub[:, 0].reshape(1, N_SUB), ((0, 0), (0, LANES - N_SUB)))

    def page(pg, col):
        return lambda b, j, pt: (pt[b * n_pages + PAGES_PER_STEP * j + pg], layer, 0, col)

    tok = lambda b, j, pt: (b, 0, 0)
    const2 = lambda b, j, pt: (0, 0)
    grid_spec = pltpu.PrefetchScalarGridSpec(
        num_scalar_prefetch=1,
        grid=(db, nstep),
        in_specs=([pl.BlockSpec((None, 1, MIX_W), tok)] * 3
                  + [pl.BlockSpec((None, None, PAGE_SIZE, MIX_W), page(pg, 0)) for pg in range(PAGES_PER_STEP)]
                  + [pl.BlockSpec((None, None, PAGE_SIZE, DIFF_W), page(pg, 1)) for pg in range(PAGES_PER_STEP)]
                  + [pl.BlockSpec((MOBA_BLOCK, LANES), const2), pl.BlockSpec((1, LANES), const2),
                     pl.BlockSpec((4, HEAD_DIM), const2), pl.BlockSpec((1, DIFF_VDIM), const2)]),
        out_specs=[pl.BlockSpec((None, 1, DIFF_W), tok),
                   pl.BlockSpec((None, MOBA_HEADS, LANES), tok)],
        scratch_shapes=[pltpu.VMEM((DIFF_W, LANES), BF16), pltpu.VMEM((nblk, MOBA_W), F32),
                        pltpu.VMEM((1, LANES), F32), pltpu.VMEM((SUBLANES, LANES), F32),
                        pltpu.VMEM((N_SUB, SUBLANES, DIFF_VDIM), F32)],
    )
    o_d, sel = pl.pallas_call(
        functools.partial(_sample_main_kernel, lam_init=lam_init),
        grid_spec=grid_spec,
        out_shape=[jax.ShapeDtypeStruct((db, 1, DIFF_W), F32),
                   jax.ShapeDtypeStruct((db, MOBA_HEADS, LANES), jnp.int32)],
        compiler_params=_cparams(("parallel", "arbitrary")),
        name="sample_main",
    )(page_table.reshape(-1), q_s, k_s, v_s, *([cache_k] * PAGES_PER_STEP), *([cache_v] * PAGES_PER_STEP),
      tail, bias_self, lam_vecs, subln_g)
    return o_d.reshape(db, DIFF_W), sel[:, :, :MOBA_TOPK]


def _sample_moba_kernel(pt_ref, sel_ref, q_ref, knew_ref, vnew_ref, *refs):
    del pt_ref, sel_ref
    n_pg = MOBA_TOPK * (MOBA_BLOCK // PAGE_SIZE)
    k_refs = refs[:n_pg]
    v_refs = refs[n_pg:2 * n_pg]
    bias_refs = refs[2 * n_pg:2 * n_pg + MOBA_TOPK]
    bias0_ref, o_ref = refs[2 * n_pg + MOBA_TOPK:]
    h = pl.program_id(1)
    par = h % 2
    lane = lax.broadcasted_iota(jnp.int32, (1, LANES), 1)
    mine = (lane >= par * HEAD_DIM) & (lane < (par + 1) * HEAD_DIM)
    qm = jnp.where(mine, q_ref[...], 0.0) * ATTN_SCALE
    q_rep = jnp.broadcast_to(qm, (LANES, LANES)).T.astype(BF16)
    m = jnp.full((1, LANES), M_INIT, F32)
    l8 = jnp.zeros((SUBLANES, LANES), F32)
    acc = jnp.zeros((SUBLANES, LANES), F32)
    ppb = MOBA_BLOCK // PAGE_SIZE
    for pg in range(n_pg):
        brow = bias_refs[pg // ppb][:, (pg % ppb) * PAGE_SIZE:(pg % ppb + 1) * PAGE_SIZE]
        s = _dot(k_refs[pg][...], q_rep)
        s = s + jnp.broadcast_to(brow, (PAGE_SIZE, PAGE_SIZE)).T
        m_new = jnp.maximum(m, _colmax(s))
        p = jnp.exp(s - m_new)
        alpha = jnp.exp(m - m_new)
        l8 = alpha * l8 + _fold8(p)
        acc = alpha * acc + _fold8(_mxu_round(p) * _mxu_round(v_refs[pg][...]))
        m = m_new
    s_self = jnp.sum(_mxu_round(qm) * _mxu_round(knew_ref[...]), axis=-1, keepdims=True) + bias0_ref[:, 0:1]
    m_new = jnp.maximum(m, s_self)
    alpha = jnp.exp(m - m_new)
    p_self = jnp.exp(s_self - m_new)
    o = ((alpha * jnp.sum(acc, axis=0, keepdims=True) + _mxu_round(p_self) * _mxu_round(vnew_ref[...]))
         / (alpha * jnp.sum(l8, axis=0, keepdims=True) + p_self))
    o_ref[...] = jnp.where(par == 0, o[:, :HEAD_DIM], o[:, HEAD_DIM:])


def _sample_moba(page_table, sel, q_s, k_s, v_s, cache_k, cache_v, layer, tab_m):
    db, n_pages = page_table.shape
    past = n_pages * PAGE_SIZE
    ppb = MOBA_BLOCK // PAGE_SIZE
    n_pg = MOBA_TOPK * ppb
    bias_past = _bias_lookup(tab_m, past - jnp.arange(past)).reshape(MOBA_HEADS, 1, past)
    bias_self = jnp.broadcast_to(tab_m[:, 0:1, None], (MOBA_HEADS, 1, LANES))

    def sel_at(b, h, t, sl):
        return sl[(b * MOBA_HEADS + h) * MOBA_TOPK + t]

    def page(pg):
        return lambda b, h, pt, sl: (pt[b * n_pages + ppb * sel_at(b, h, pg // ppb, sl) + pg % ppb], layer, 0, h // 2)

    def bias_blk(t):
        return lambda b, h, pt, sl: (h, 0, sel_at(b, h, t, sl))

    tok = lambda b, h, pt, sl: (b, 0, h // 2)
    grid_spec = pltpu.PrefetchScalarGridSpec(
        num_scalar_prefetch=2,
        grid=(db, MOBA_HEADS),
        in_specs=([pl.BlockSpec((None, 1, LANES), tok)] * 3
                  + [pl.BlockSpec((None, None, PAGE_SIZE, LANES), page(pg)) for pg in range(n_pg)] * 2
                  + [pl.BlockSpec((None, 1, MOBA_BLOCK), bias_blk(t)) for t in range(MOBA_TOPK)]
                  + [pl.BlockSpec((None, 1, LANES), lambda b, h, pt, sl: (h, 0, 0))]),
        out_specs=pl.BlockSpec((None, None, 1, HEAD_DIM), lambda b, h, pt, sl: (b, h, 0, 0)),
    )
    o_m = pl.pallas_call(
        _sample_moba_kernel,
        grid_spec=grid_spec,
        out_shape=jax.ShapeDtypeStruct((db, MOBA_HEADS, 1, HEAD_DIM), F32),
        compiler_params=_cparams(("parallel", "arbitrary")),
        name="sample_moba",
    )(page_table.reshape(-1), sel.reshape(-1), q_s, k_s, v_s,
      *([cache_k] * n_pg), *([cache_v] * n_pg), *([bias_past] * MOBA_TOPK), bias_self)
    return o_m.reshape(db, MOBA_W)


def _post_attn_kernel(h_ref, mm_ref, md_ref, wo_ref, g_ref, b_ref, wr_ref, br_ref,
                      a_ref, eid_ref, gate_ref, *, alpha, transposed):
    if transposed:
        mm = mm_ref[...].astype(F32).T
        md = md_ref[...].astype(F32).T
    else:
        mm, md = mm_ref[...], md_ref[...]
    mix = _dot(mm, wo_ref[:MOBA_W, :]) + _dot(md, wo_ref[MOBA_W:, :])
    a = _layer_norm(alpha * h_ref[...] + mix, g_ref[...], b_ref[...])
    _store_tiles(a_ref, a)
    logits = _dot(a, wr_ref[...]) + br_ref[...]
    lane = lax.broadcasted_iota(jnp.int32, logits.shape, 1).astype(F32)
    none = float(LANES)
    gl = jnp.where(lane < N_GROUPS, logits, NEG_INF)
    gmax = jnp.max(gl, axis=-1, keepdims=True)
    g_gate = 1.0 / jnp.sum(jnp.exp(gl - gmax), axis=-1, keepdims=True)
    g_idx = jnp.min(jnp.where(gl == gmax, lane, none), axis=-1, keepdims=True)
    lo = N_GROUPS + EXPERTS_PER_GROUP * g_idx
    el = jnp.where(lane >= lo, jnp.where(lane < lo + EXPERTS_PER_GROUP, logits, NEG_INF), NEG_INF)
    m1 = jnp.max(el, axis=-1, keepdims=True)
    i1 = jnp.min(jnp.where(el == m1, lane, none), axis=-1, keepdims=True)
    el2 = jnp.where(lane == i1, NEG_INF, el)
    m2 = jnp.max(el2, axis=-1, keepdims=True)
    i2 = jnp.min(jnp.where(el2 == m2, lane, none), axis=-1, keepdims=True)
    r = jnp.exp(m2 - m1)
    w1 = g_gate / (1.0 + r)
    w2 = g_gate * r / (1.0 + r)
    eid_ref[...] = jnp.where(lane == 0.0, i1 - N_GROUPS, jnp.where(lane == 1.0, i2 - N_GROUPS, 0.0)).astype(jnp.int32)
    gate_ref[...] = jnp.where(lane == 0.0, w1, jnp.where(lane == 1.0, w2, 0.0))


def _post_attn(h, mix_m, mix_d, w_out, ln_g, ln_b, w_router, b_router, alpha, tm, transposed):
    m = h.shape[0]
    row = lambda i: (i, 0)
    const = lambda i: (0, 0)
    if transposed:
        assert tm == T
        mix_specs = [pl.BlockSpec((None, MOBA_W, T), lambda i: (i, 0, 0)),
                     pl.BlockSpec((None, DIFF_W, T), lambda i: (i, 0, 0))]
    else:
        mix_specs = [pl.BlockSpec((tm, MOBA_W), row), pl.BlockSpec((tm, DIFF_W), row)]
    return pl.pallas_call(
        functools.partial(_post_attn_kernel, alpha=alpha, transposed=transposed),
        grid=(m // tm,),
        in_specs=[pl.BlockSpec((tm, D_MODEL), row)] + mix_specs + [
            pl.BlockSpec((MIX_W, D_MODEL), const),
            pl.BlockSpec((1, D_MODEL), const), pl.BlockSpec((1, D_MODEL), const),
            pl.BlockSpec((D_MODEL, LANES), const), pl.BlockSpec((1, LANES), const)],
        out_specs=[pl.BlockSpec((tm * N_CHUNK, LANES), row), pl.BlockSpec((tm, LANES), row),
                   pl.BlockSpec((tm, LANES), row)],
        out_shape=[jax.ShapeDtypeStruct((m * N_CHUNK, LANES), F32), jax.ShapeDtypeStruct((m, LANES), jnp.int32),
                   jax.ShapeDtypeStruct((m, LANES), F32)],
        compiler_params=_cparams(("parallel",)),
        name="post_attn",
    )(h, mix_m, mix_d, w_out, ln_g, ln_b, w_router, b_router)


def _moe_kernel(code_ref, blk_e_ref, blk_src_ref, blk_n_ref, nblk_ref, a_hbm, w1_ref, w3_ref, w2_ref, y_hbm,
                xbuf, ybuf, w1b, w3b, w2b, gsem, ssem, *, rb, n_tok):
    i = pl.program_id(0)
    n_used = nblk_ref[0]

    def tile(t):
        start = t * N_CHUNK
        return pl.ds(start if isinstance(t, int) else pl.multiple_of(start, N_CHUNK), N_CHUNK)

    def gather_row(base, r, slot):
        tok = code_ref[base + r] >> 1
        return pltpu.make_async_copy(a_hbm.at[tile(tok)], xbuf.at[slot, tile(r)], gsem.at[slot])

    def scatter_row(base, r, n_valid, slot):
        code = code_ref[base + r]
        dst = jnp.where(r < n_valid, (code >> 1) + (code & 1) * n_tok, 2 * n_tok + slot * rb + r)
        return pltpu.make_async_copy(ybuf.at[slot, tile(r)], y_hbm.at[tile(dst)], ssem.at[slot])

    def for_rows(fn):
        def body(r, carry):
            fn(r)
            return carry
        lax.fori_loop(0, rb, body, 0, unroll=SUBLANES)

    def wait_gather(slot):
        pltpu.make_async_copy(xbuf.at[slot], xbuf.at[slot], gsem.at[slot]).wait()

    def wait_scatter(slot):
        pltpu.make_async_copy(ybuf.at[slot], ybuf.at[slot], ssem.at[slot]).wait()

    def step(slot):
        other = 1 - slot

        def first_step():
            base0 = blk_src_ref[0]
            for_rows(lambda r: gather_row(base0, r, 0).start())
            ybuf[1] = jnp.zeros(ybuf.shape[1:], F32)
            for half in range(2):
                for_rows(lambda r, half=half: pltpu.make_async_copy(
                    ybuf.at[1, tile(r)], y_hbm.at[tile(2 * n_tok + half * rb + r)], ssem.at[1]).start())
                wait_scatter(1)

        if slot == 0:
            pl.when(i == 0)(first_step)

        e = blk_e_ref[i]

        @pl.when((i == 0) | (e != blk_e_ref[jnp.maximum(i - 1, 0)]))
        def _():
            w1b[...] = w1_ref[...].astype(BF16)
            w3b[...] = w3_ref[...].astype(BF16)
            w2b[...] = w2_ref[...].astype(BF16)

        wait_gather(slot)

        @pl.when(i >= 1)
        def _():
            wait_scatter(slot)

        nxt_base = blk_src_ref[jnp.minimum(i + 1, pl.num_programs(0) - 1)]
        prv = jnp.maximum(i - 1, 0)
        prv_base = blk_src_ref[prv]
        prv_valid = jnp.where(i >= 1, blk_n_ref[prv], 0)

        def issue_rows(lo, hi):
            for r in range(lo, hi):
                gather_row(nxt_base, r, other).start()
                scatter_row(prv_base, r, prv_valid, other).start(priority=1)

        cut1, cut2 = rb // 3, 2 * rb // 3
        x = _load_tiles(xbuf.at[slot]).astype(BF16)
        h1 = jnp.dot(x, w1b[...], preferred_element_type=F32)
        issue_rows(0, cut1)
        h3 = jnp.dot(x, w3b[...], preferred_element_type=F32)
        issue_rows(cut1, cut2)
        hidden = (h1 * jax.nn.sigmoid(h1) * h3).astype(BF16)
        y = jnp.dot(hidden, w2b[...], preferred_element_type=F32)
        issue_rows(cut2, rb)
        _store_tiles(ybuf.at[slot], y)

        @pl.when(i == n_used - 1)
        def _():
            base = blk_src_ref[i]
            n_valid = blk_n_ref[i]
            for_rows(lambda r: scatter_row(base, r, n_valid, slot).start())
            wait_scatter(slot)
            wait_scatter(other)
            wait_gather(other)

    for parity in range(2):
        pl.when((i < n_used) & (i % 2 == parity))(functools.partial(step, parity))


def _route_tables(eid, rb):
    n = eid.shape[0]
    m = n * 2
    e_flat = eid.reshape(m)
    keys = lax.sort(e_flat * m + jnp.arange(m, dtype=jnp.int32))
    code = keys % m
    experts = jnp.arange(N_EXPERTS, dtype=jnp.int32)
    counts = jnp.sum((e_flat[:, None] == experts[None, :]).astype(jnp.int32), axis=0)
    padded = (counts + rb - 1) // rb * rb
    start = jnp.cumsum(counts) - counts
    p_end = jnp.cumsum(padded)
    p_start = p_end - padded
    n_blk = -(-(m + N_EXPERTS * (rb - 1)) // rb)
    blk_row = jnp.arange(n_blk, dtype=jnp.int32) * rb
    blk_e = jnp.minimum(jnp.sum((p_end[None, :] <= blk_row[:, None]).astype(jnp.int32), axis=1), N_EXPERTS - 1)
    onehot = (blk_e[:, None] == experts[None, :]).astype(jnp.int32)
    pick = lambda table: jnp.sum(onehot * table[None, :], axis=1)
    off = blk_row - pick(p_start)
    blk_src = jnp.clip(pick(start) + off, 0, m - 1).astype(jnp.int32)
    blk_n = jnp.clip(pick(counts) - off, 0, rb).astype(jnp.int32)
    n_used = (p_end[-1:] // rb).astype(jnp.int32)
    code = jnp.pad(code.astype(jnp.int32), (0, rb))
    return code, blk_e.astype(jnp.int32), blk_src, blk_n, n_used, n_blk


def _moe(a_tiles, eid, w1, w3, w2, rb):
    n = a_tiles.shape[0] // N_CHUNK
    code, blk_e, blk_src, blk_n, n_used, n_blk = _route_tables(eid, rb)
    wmap = lambda i, code, be, bs, bn, nu: (be[i], 0, 0)
    grid_spec = pltpu.PrefetchScalarGridSpec(
        num_scalar_prefetch=5,
        grid=(n_blk,),
        in_specs=[pl.BlockSpec(memory_space=pl.ANY),
                  pl.BlockSpec((None, D_MODEL, D_EXPERT), wmap),
                  pl.BlockSpec((None, D_MODEL, D_EXPERT), wmap),
                  pl.BlockSpec((None, D_EXPERT, D_MODEL), wmap)],
        out_specs=pl.BlockSpec(memory_space=pl.ANY),
        scratch_shapes=[pltpu.VMEM((2, rb * N_CHUNK, LANES), F32), pltpu.VMEM((2, rb * N_CHUNK, LANES), F32),
                        pltpu.VMEM((D_MODEL, D_EXPERT), BF16), pltpu.VMEM((D_MODEL, D_EXPERT), BF16),
                        pltpu.VMEM((D_EXPERT, D_MODEL), BF16),
                        pltpu.SemaphoreType.DMA((2,)), pltpu.SemaphoreType.DMA((2,))],
    )
    return pl.pallas_call(
        functools.partial(_moe_kernel, rb=rb, n_tok=n),
        grid_spec=grid_spec,
        out_shape=jax.ShapeDtypeStruct(((2 * n + 2 * rb) * N_CHUNK, LANES), F32),
        compiler_params=_cparams(("arbitrary",)),
        name="moe",
    )(code, blk_e, blk_src, blk_n, n_used, a_tiles, w1, w3, w2)


def _finish_kernel(a_ref, y0_ref, y1_ref, gate_ref, p_ref, g_ref, b_ref, wpg_ref, bpg_ref, wple_ref, o_ref,
                   *, alpha):
    gates = gate_ref[...]
    moe = _load_tiles(y0_ref) * gates[:, 0:1] + _load_tiles(y1_ref) * gates[:, 1:2]
    c = _layer_norm(alpha * _load_tiles(a_ref) + moe, g_ref[...], b_ref[...])
    gate = jax.nn.sigmoid(_dot(c, wpg_ref[...]) + bpg_ref[...])
    o_ref[...] = c + gate * _dot(p_ref[...], wple_ref[...])


def _finish(a, y, gates, p, ln_g, ln_b, w_pg, b_pg, w_ple, alpha, tm):
    m = a.shape[0] // N_CHUNK
    assert m % tm == 0
    row = lambda i: (i, 0)
    const = lambda i: (0, 0)
    tiles = (tm * N_CHUNK, LANES)
    return pl.pallas_call(
        functools.partial(_finish_kernel, alpha=alpha),
        grid=(m // tm,),
        in_specs=[pl.BlockSpec(tiles, row), pl.BlockSpec(tiles, row),
                  pl.BlockSpec(tiles, lambda i: (m // tm + i, 0)),
                  pl.BlockSpec((tm, LANES), row), pl.BlockSpec((tm, D_PLE), row),
                  pl.BlockSpec((1, D_MODEL), const), pl.BlockSpec((1, D_MODEL), const),
                  pl.BlockSpec((D_MODEL, D_MODEL), const), pl.BlockSpec((1, D_MODEL), const),
                  pl.BlockSpec((D_PLE, D_MODEL), const)],
        out_specs=pl.BlockSpec((tm, D_MODEL), row),
        out_shape=jax.ShapeDtypeStruct((m, D_MODEL), F32),
        compiler_params=_cparams(("parallel",)),
        name="finish",
    )(a, y, y, gates, p, ln_g, ln_b, w_pg, b_pg, w_ple)


def _pack_router(w_group, b_group, w_er, b_er):
    w = jnp.concatenate([w_group, jnp.transpose(w_er, (1, 0, 2)).reshape(D_MODEL, N_EXPERTS)], axis=1)
    b = jnp.concatenate([b_group, b_er.reshape(N_EXPERTS)])
    pad = LANES - w.shape[1]
    return jnp.pad(w, ((0, 0), (0, pad))), jnp.pad(b, (0, pad)).reshape(1, LANES)


def kernel(x_prompt, x_sample, cache_k, cache_v, page_table, p_prompt, p_sample, rel_bias, w_in, w_out, lam_q1, lam_k1, lam_q2, lam_k2, subln_g, ln1_g, ln1_b, w_group, b_group, w_erouter, b_erouter, w1, w3, w2, ln2_g, ln2_b, w_ple, w_pg, b_pg):
    depth = w_in.shape[0]
    b, s, _ = x_prompt.shape
    db, ds, _ = x_sample.shape
    assert ds == 1 and s % T == 0
    alpha = (2 * depth) ** 0.25
    tab_m = rel_bias[:, :MOBA_HEADS].T
    tab_d = rel_bias[:, MOBA_HEADS:].T
    hp = x_prompt.reshape(b * s, D_MODEL)
    hs = x_sample.reshape(db, D_MODEL)
    kp_rows, vp_rows, ks_rows, vs_rows = [], [], [], []
    for i in range(depth):
        lam_init = 0.8 - 0.6 * math.exp(-0.3 * i)
        lam_vecs = jnp.stack([lam_q1[i], lam_k1[i], lam_q2[i], lam_k2[i]]).astype(F32)
        g_sub = subln_g[i].reshape(1, DIFF_VDIM)
        w_router, b_router = _pack_router(w_group[i], b_group[i], w_erouter[i], b_erouter[i])
        ln1 = (ln1_g[i].reshape(1, D_MODEL), ln1_b[i].reshape(1, D_MODEL))
        ln2 = (ln2_g[i].reshape(1, D_MODEL), ln2_b[i].reshape(1, D_MODEL))
        b_pg_i = b_pg[i].reshape(1, D_MODEL)

        w_in_b, w_out_b = w_in[i].astype(BF16), w_out[i].astype(BF16)
        w_pg_b, w_ple_b, w_router_b = w_pg[i].astype(BF16), w_ple[i].astype(BF16), w_router.astype(BF16)

        k_p, v_p, qt, kb, vt, kmean = _qkv_prompt(hp, w_in_b)
        mix_m = _moba_prompt(qt, kb, vt, kmean, tab_m, b, s)
        mix_d = _diff_prompt(qt, kb, vt, tab_d, lam_vecs, g_sub, lam_init, b, s)
        a_p, eid_p, gate_p = _post_attn(hp, mix_m, mix_d, w_out_b, *ln1, w_router_b, b_router, alpha, T, True)
        y_p = _moe(a_p, eid_p[:, :2], w1[i], w3[i], w2[i], 256)
        hp = _finish(a_p, y_p, gate_p, p_prompt[i].reshape(b * s, D_PLE), *ln2, w_pg_b, b_pg_i, w_ple_b,
                     alpha, 256)

        qkv_s = _mm_small(hs, w_in_b, MIX_W)
        q_s = qkv_s[:, :MIX_W].reshape(db, 1, MIX_W)
        k_s = qkv_s[:, MIX_W:2 * MIX_W].reshape(db, 1, MIX_W)
        v_s = qkv_s[:, 2 * MIX_W:].reshape(db, 1, MIX_W)
        o_d, sel = _sample_main(page_table, q_s, k_s, v_s, cache_k, cache_v, i, tab_d, lam_vecs, g_sub, lam_init)
        o_m = _sample_moba(page_table, sel, q_s, k_s, v_s, cache_k, cache_v, i, tab_m)
        a_s, eid_s, gate_s = _post_attn(hs, o_m, o_d, w_out_b, *ln1, w_router_b, b_router, alpha, db, False)
        y_s = _moe(a_s, eid_s[:, :2], w1[i], w3[i], w2[i], 64)
        hs = _finish(a_s, y_s, gate_s, p_sample[i].reshape(db, D_PLE), *ln2, w_pg_b, b_pg_i, w_ple_b, alpha, db)

        kp_rows.append(k_p.reshape(b, s, MIX_W))
        vp_rows.append(v_p.reshape(b, s, MIX_W))
        ks_rows.append(k_s.reshape(db, ds, MIX_W))
        vs_rows.append(v_s.reshape(db, ds, MIX_W))
    return (hp.reshape(b, s, D_MODEL), hs.reshape(db, ds, D_MODEL),
            jnp.stack(kp_rows, axis=1), jnp.stack(vp_rows, axis=1),
            jnp.stack(ks_rows, axis=1), jnp.stack(vs_rows, axis=1))
```

```python
import functools
import math

import jax
import jax.numpy as jnp
from jax import lax
from jax.experimental import pallas as pl
from jax.experimental.pallas import tpu as pltpu

D_MODEL = 1024
PAGE_SIZE = 128
HEAD_DIM = 64
MOBA_HEADS = 8
MOBA_W = MOBA_HEADS * HEAD_DIM
DIFF_HEADS = 4
DIFF_VDIM = 2 * HEAD_DIM
DIFF_W = DIFF_HEADS * DIFF_VDIM
N_SUB = 2 * DIFF_HEADS
MIX_W = MOBA_W + DIFF_W
MOBA_BLOCK = 256
MOBA_TOPK = 3
N_BUCKETS = 32
REL_MAX_DIST = 128
N_GROUPS = 4
EXPERTS_PER_GROUP = 8
N_EXPERTS = N_GROUPS * EXPERTS_PER_GROUP
D_EXPERT = D_MODEL // 2
D_PLE = 256
LN_EPS = 1e-5
ATTN_SCALE = HEAD_DIM ** -0.5
LANES = 128
SUBLANES = 8
N_CHUNK = D_MODEL // LANES
PAGES_PER_STEP = 8
F32 = jnp.float32
BF16 = jnp.bfloat16
HIGHEST = lax.Precision.HIGHEST
NEG_INF = float("-inf")
M_INIT = -1e30
VMEM_LIMIT = 48 * 1024 * 1024
T = MOBA_BLOCK


def _cparams(sem):
    return pltpu.CompilerParams(dimension_semantics=sem, vmem_limit_bytes=VMEM_LIMIT)


def _dot(a, b):
    return jnp.dot(a.astype(BF16), b.astype(BF16), preferred_element_type=F32)


def _mxu_round(x):
    return x.astype(BF16).astype(F32)


def _rel_bucket(rel):
    n = jnp.maximum(rel, 0)
    max_exact = N_BUCKETS // 2
    large = max_exact + (jnp.log(jnp.maximum(n, 1).astype(F32) / max_exact)
                         / math.log(REL_MAX_DIST / max_exact) * (N_BUCKETS - max_exact)).astype(jnp.int32)
    large = jnp.minimum(large, N_BUCKETS - 1)
    return jnp.where(n < max_exact, n, large)


def _bias_lookup(tab, rel):
    onehot = (_rel_bucket(rel)[..., None] == jnp.arange(N_BUCKETS)).astype(F32)
    return jnp.einsum('hb,...b->h...', tab, onehot, precision=HIGHEST)


def _layer_norm(x, g, b):
    xc = x - jnp.mean(x, -1, keepdims=True)
    var = jnp.mean(xc * xc, -1, keepdims=True)
    return xc * lax.rsqrt(var + LN_EPS) * g + b


def _fold8(x):
    return jnp.sum(x.reshape(x.shape[0] // SUBLANES, SUBLANES, x.shape[1]), axis=0)


def _store_tiles(ref, x):
    rows = x.shape[0]
    for j in range(N_CHUNK):
        ref[pl.ds(j, rows, stride=N_CHUNK), :] = x[:, j * LANES:(j + 1) * LANES]


def _load_tiles(ref):
    rows = ref.shape[0] // N_CHUNK
    return jnp.concatenate([ref[pl.ds(j, rows, stride=N_CHUNK), :] for j in range(N_CHUNK)], axis=1)


def _colmax(x):
    part = jnp.max(x.reshape(x.shape[0] // SUBLANES, SUBLANES, x.shape[1]), axis=0)
    return jnp.max(part, axis=0, keepdims=True)


def _qkv_prompt_kernel(x_ref, w_ref, k_ref, v_ref, qt_ref, kb_ref, vt_ref, kmean_ref):
    x = x_ref[...].astype(BF16)
    q = jnp.dot(x, w_ref[:, :MIX_W], preferred_element_type=F32)
    qt_ref[...] = q.T.astype(BF16)
    k = jnp.dot(x, w_ref[:, MIX_W:2 * MIX_W], preferred_element_type=F32)
    k_ref[...] = k
    kb_ref[...] = k.astype(BF16)
    kmean_ref[...] = jnp.sum(k[:, :MOBA_W], axis=0, keepdims=True) * (1.0 / MOBA_BLOCK)
    v = jnp.dot(x, w_ref[:, 2 * MIX_W:], preferred_element_type=F32)
    v_ref[...] = v
    vt_ref[...] = v.T.astype(BF16)


def _qkv_prompt(x, w_bf16):
    m = x.shape[0]
    nt = m // T
    row = lambda i: (i, 0)
    tile3 = lambda i: (i, 0, 0)
    return pl.pallas_call(
        _qkv_prompt_kernel,
        grid=(nt,),
        in_specs=[pl.BlockSpec((T, D_MODEL), row),
                  pl.BlockSpec((D_MODEL, 3 * MIX_W), lambda i: (0, 0))],
        out_specs=[pl.BlockSpec((T, MIX_W), row), pl.BlockSpec((T, MIX_W), row),
                   pl.BlockSpec((None, MIX_W, T), tile3), pl.BlockSpec((T, MIX_W), row),
                   pl.BlockSpec((None, MIX_W, T), tile3),
                   pl.BlockSpec((None, 1, MOBA_W), tile3)],
        out_shape=[jax.ShapeDtypeStruct((m, MIX_W), F32), jax.ShapeDtypeStruct((m, MIX_W), F32),
                   jax.ShapeDtypeStruct((nt, MIX_W, T), BF16), jax.ShapeDtypeStruct((m, MIX_W), BF16),
                   jax.ShapeDtypeStruct((nt, MIX_W, T), BF16),
                   jax.ShapeDtypeStruct((nt, 1, MOBA_W), F32)],
        compiler_params=_cparams(("parallel",)),
        name="qkv_prompt",
    )(x, w_bf16)


def _mm_small_kernel(x_ref, w_ref, o_ref):
    o_ref[...] = _dot(x_ref[...], w_ref[...])


def _mm_small(x, w, tn):
    m, k = x.shape
    n = w.shape[1]
    return pl.pallas_call(
        _mm_small_kernel,
        grid=(n // tn,),
        in_specs=[pl.BlockSpec((m, k), lambda j: (0, 0)), pl.BlockSpec((k, tn), lambda j: (0, j))],
        out_specs=pl.BlockSpec((m, tn), lambda j: (0, j)),
        out_shape=jax.ShapeDtypeStruct((m, n), F32),
        compiler_params=_cparams(("parallel",)),
        name="mm_small",
    )(x, w)


def _stream_blocks(i, n_streams, scores, values, keep_row, bias_d, bias_p,
                   m_ref, l_ref, acc_ref, s_ref, p_ref, a_ref, dv):
    m_ref[...] = jnp.full(m_ref.shape, M_INIT, F32)
    l_ref[...] = jnp.zeros(l_ref.shape, F32)
    acc_ref[...] = jnp.zeros(acc_ref.shape, F32)

    def block(j, finish_scores, shift_of, gated, stage):
        col_max = []
        for r in range(n_streams):
            s = finish_scores(r, j, scores(r, j))
            s_ref[stage, r] = s
            col_max.append(_colmax(s))
        alphas = []
        for r in range(n_streams):
            shift = shift_of(r)
            m_old = m_ref[r]
            tile_max = col_max[r] + shift
            keep = keep_row(r, j) if gated else None
            if keep is not None:
                tile_max = jnp.where(keep > 0.5, tile_max, NEG_INF)
            m_new = jnp.maximum(m_old, tile_max)
            offset = m_new - shift
            if keep is not None:
                offset = jnp.where(keep > 0.5, offset, float("inf"))
            p = jnp.exp(s_ref[stage, r] - offset)
            alpha = jnp.exp(m_old - m_new)
            p_ref[stage, r] = p.astype(p_ref.dtype)
            m_ref[r] = m_new
            alphas.append(alpha)
        ones = jnp.ones((2 * SUBLANES, T), p_ref.dtype)
        for r in range(n_streams):
            rows = slice(r * dv, (r + 1) * dv)
            out = jnp.dot(jnp.concatenate([values(r, j), ones], axis=0), p_ref[stage, r],
                          preferred_element_type=F32)
            acc_ref[rows, :] = alphas[r] * acc_ref[rows, :] + out[:dv]
            l_ref[r] = alphas[r] * l_ref[r] + out[dv:dv + 1]

    n_far = jnp.maximum(i - 1, 0)
    last_far = jnp.maximum(n_far - 1, 0)

    def qk(j, slot):
        for r in range(n_streams):
            s_ref[slot, r] = scores(r, j)

    def softmax(t, slot):
        live = jnp.where(t < n_far, 0.0, NEG_INF).astype(F32).reshape(1, 1)
        for r in range(n_streams):
            shift = bias_p(r)[0:1, 0:1] + live
            s = s_ref[slot, r]
            m_old = m_ref[r]
            tile_max = _colmax(s) + shift
            keep = keep_row(r, t)
            if keep is not None:
                tile_max = jnp.where(keep > 0.5, tile_max, NEG_INF)
            m_new = jnp.maximum(m_old, tile_max)
            offset = m_new - shift
            if keep is not None:
                offset = jnp.where(keep > 0.5, offset, float("inf"))
            p_ref[slot, r] = jnp.exp(s - offset).astype(p_ref.dtype)
            a_ref[slot, r] = jnp.exp(m_old - m_new)
            m_ref[r] = m_new

    def pv(j, slot):
        ones = jnp.ones((2 * SUBLANES, T), p_ref.dtype)
        for r in range(n_streams):
            rows = slice(r * dv, (r + 1) * dv)
            out = jnp.dot(jnp.concatenate([values(r, j), ones], axis=0), p_ref[slot, r],
                          preferred_element_type=F32)
            alpha = a_ref[slot, r]
            acc_ref[rows, :] = alpha * acc_ref[rows, :] + out[:dv]
            l_ref[r] = alpha * l_ref[r] + out[dv:dv + 1]

    p_ref[1] = jnp.zeros(p_ref.shape[1:], p_ref.dtype)
    a_ref[1] = jnp.ones(a_ref.shape[1:], F32)
    qk(0, 0)

    def far_pair(u, carry):
        for slot in range(2):
            t = 2 * u + slot
            qk(jnp.minimum(t + 1, last_far), 1 - slot)
            softmax(t, slot)
            pv(jnp.maximum(t - 1, 0), 1 - slot)
        return carry

    n_pairs = (n_far + 1) // 2
    lax.fori_loop(0, n_pairs, far_pair, 0)
    pv(jnp.maximum(2 * n_pairs - 1, 0), 1)

    prev_shift = jnp.where(i >= 1, 0.0, NEG_INF).astype(F32).reshape(1, 1)
    block(jnp.maximum(i - 1, 0), lambda r, j, s: s + bias_p(r)[...], lambda r: prev_shift, True, 0)

    key = lax.broadcasted_iota(jnp.int32, (T, T), 0)
    qry = lax.broadcasted_iota(jnp.int32, (T, T), 1)
    zero = jnp.zeros((1, 1), F32)
    block(i, lambda r, j, s: jnp.where(qry >= key, s + bias_d(r)[...], NEG_INF), lambda r: zero, False, 1)


def _masked_half(qt_slab, half):
    row = lax.broadcasted_iota(jnp.int32, qt_slab.shape, 0)
    mine = (row >= half * HEAD_DIM) & (row < (half + 1) * HEAD_DIM)
    return jnp.where(mine, qt_slab, jnp.zeros_like(qt_slab))


def _top_blocks_t(gate_t, n_valid):
    nb = gate_t.shape[0]
    ids = lax.broadcasted_iota(jnp.int32, gate_t.shape, 0).astype(F32)
    g = jnp.where(ids < n_valid.astype(F32), gate_t, NEG_INF)
    sel = jnp.zeros(gate_t.shape, F32)
    for _ in range(MOBA_TOPK):
        mx = jnp.max(g, axis=0, keepdims=True)
        cand = jnp.where(g == mx, jnp.where(mx > NEG_INF, ids, float(nb)), float(nb))
        pick = ids == jnp.min(cand, axis=0, keepdims=True)
        sel = jnp.where(pick, 1.0, sel)
        g = jnp.where(pick, NEG_INF, g)
    return sel


def _moba_prompt_kernel(qt_ref, k_ref, vt_ref, kmean_ref, bd_ref, bp_ref, o_ref,
                        qs_ref, sel_ref, m_ref, l_ref, acc_ref, s_ref, p_ref, a_ref):
    i = pl.program_id(1)
    for h in range(MOBA_HEADS):
        slab = slice((h // 2) * LANES, (h // 2 + 1) * LANES)
        qh = _masked_half(qt_ref[slab, :], h % 2)
        gate_t = jnp.dot(kmean_ref[:, slab].astype(BF16), qh, preferred_element_type=F32)
        sel_ref[h] = _top_blocks_t(gate_t, i)
        qs_ref[h] = qh * ATTN_SCALE

    def scores(h, j):
        rows = pl.ds(pl.multiple_of(j * T, T), T)
        slab = slice((h // 2) * LANES, (h // 2 + 1) * LANES)
        return jnp.dot(k_ref[rows, slab], qs_ref[h], preferred_element_type=F32)

    _stream_blocks(i, MOBA_HEADS, scores,
                   lambda h, j: vt_ref[j, h * HEAD_DIM:(h + 1) * HEAD_DIM, :],
                   lambda h, j: sel_ref[h, pl.ds(j, 1), :],
                   lambda h: bd_ref.at[h], lambda h: bp_ref.at[h], m_ref, l_ref, acc_ref, s_ref, p_ref, a_ref,
                   HEAD_DIM)
    for h in range(MOBA_HEADS):
        rows = slice(h * HEAD_DIM, (h + 1) * HEAD_DIM)
        o_ref[rows, :] = (acc_ref[rows, :] / l_ref[h]).astype(o_ref.dtype)


def _lambda(lam_ref, lam_init):
    lv = lam_ref[...]
    s1 = jnp.sum(lv[0:1] * lv[1:2], axis=-1, keepdims=True)
    s2 = jnp.sum(lv[2:3] * lv[3:4], axis=-1, keepdims=True)
    return jnp.exp(s1) - jnp.exp(s2) + lam_init


def _diff_prompt_kernel(qt_ref, k_ref, vt_ref, bd_ref, bp_ref, lam_ref, g_ref, o_ref,
                        qs_ref, m_ref, l_ref, acc_ref, s_ref, p_ref, a_ref, *, lam_init):
    i = pl.program_id(1)
    for r in range(N_SUB):
        slab = slice((r // 2) * LANES, (r // 2 + 1) * LANES)
        qs_ref[r] = _masked_half(qt_ref[slab, :], r % 2) * ATTN_SCALE

    def scores(r, j):
        rows = pl.ds(pl.multiple_of(j * T, T), T)
        slab = slice((r // 2) * LANES, (r // 2 + 1) * LANES)
        return jnp.dot(k_ref[rows, slab], qs_ref[r], preferred_element_type=F32)

    _stream_blocks(i, N_SUB, scores,
                   lambda r, j: vt_ref[j, (r // 2) * DIFF_VDIM:(r // 2 + 1) * DIFF_VDIM, :],
                   lambda r, j: None,
                   lambda r: bd_ref.at[r // 2], lambda r: bp_ref.at[r // 2], m_ref, l_ref, acc_ref, s_ref, p_ref, a_ref,
                   DIFF_VDIM)
    lam = _lambda(lam_ref, lam_init)
    for h in range(DIFF_HEADS):
        parts = []
        for c in range(2):
            r = 2 * h + c
            parts.append(acc_ref[r * DIFF_VDIM:(r + 1) * DIFF_VDIM, :] / l_ref[r])
        o = parts[0] - lam * parts[1]
        o = o * lax.rsqrt(jnp.mean(o * o, axis=0, keepdims=True) + LN_EPS) * g_ref[...] * (1.0 - lam_init)
        o_ref[h * DIFF_VDIM:(h + 1) * DIFF_VDIM, :] = o.astype(o_ref.dtype)


def _prompt_bias_tiles(tab):
    key = jnp.arange(T)[:, None]
    qry = jnp.arange(T)[None, :]
    return _bias_lookup(tab, qry - key), _bias_lookup(tab, qry - key + T)


def _moba_prompt(qt, kb, vt, kmean, tab_m, b, s):
    nq = s // T
    bd, bp = _prompt_bias_tiles(tab_m)
    const3 = lambda bi, i: (0, 0, 0)
    return pl.pallas_call(
        _moba_prompt_kernel,
        grid=(b, nq),
        in_specs=[pl.BlockSpec((None, MOBA_W, T), lambda bi, i: (bi * nq + i, 0, 0)),
                  pl.BlockSpec((s, MOBA_W), lambda bi, i: (bi, 0)),
                  pl.BlockSpec((nq, MOBA_W, T), lambda bi, i: (bi, 0, 0)),
                  pl.BlockSpec((None, nq, MOBA_W), lambda bi, i: (bi, 0, 0)),
                  pl.BlockSpec((MOBA_HEADS, T, T), const3),
                  pl.BlockSpec((MOBA_HEADS, T, T), const3)],
        out_specs=pl.BlockSpec((None, MOBA_W, T), lambda bi, i: (bi * nq + i, 0, 0)),
        out_shape=jax.ShapeDtypeStruct((b * nq, MOBA_W, T), BF16),
        scratch_shapes=[pltpu.VMEM((MOBA_HEADS, LANES, T), BF16), pltpu.VMEM((MOBA_HEADS, nq, T), F32),
                        pltpu.VMEM((MOBA_HEADS, 1, T), F32), pltpu.VMEM((MOBA_HEADS, 1, T), F32),
                        pltpu.VMEM((MOBA_W, T), F32),
                        pltpu.VMEM((2, MOBA_HEADS, T, T), F32), pltpu.VMEM((2, MOBA_HEADS, T, T), BF16),
                        pltpu.VMEM((2, MOBA_HEADS, 1, T), F32)],
        compiler_params=_cparams(("parallel", "arbitrary")),
        name="moba_prompt",
    )(qt, kb, vt, kmean.reshape(b, nq, MOBA_W), bd, bp)


def _diff_prompt(qt, kb, vt, tab_d, lam_vecs, subln_g, lam_init, b, s):
    nq = s // T
    bd, bp = _prompt_bias_tiles(tab_d)
    g_cols = jnp.broadcast_to(subln_g.reshape(DIFF_VDIM, 1), (DIFF_VDIM, T))
    const3 = lambda bi, i: (0, 0, 0)
    const2 = lambda bi, i: (0, 0)
    return pl.pallas_call(
        functools.partial(_diff_prompt_kernel, lam_init=lam_init),
        grid=(b, nq),
        in_specs=[pl.BlockSpec((None, DIFF_W, T), lambda bi, i: (bi * nq + i, 1, 0)),
                  pl.BlockSpec((s, DIFF_W), lambda bi, i: (bi, 1)),
                  pl.BlockSpec((nq, DIFF_W, T), lambda bi, i: (bi, 1, 0)),
                  pl.BlockSpec((DIFF_HEADS, T, T), const3),
                  pl.BlockSpec((DIFF_HEADS, T, T), const3),
                  pl.BlockSpec((4, HEAD_DIM), const2),
                  pl.BlockSpec((DIFF_VDIM, T), const2)],
        out_specs=pl.BlockSpec((None, DIFF_W, T), lambda bi, i: (bi * nq + i, 0, 0)),
        out_shape=jax.ShapeDtypeStruct((b * nq, DIFF_W, T), BF16),
        scratch_shapes=[pltpu.VMEM((N_SUB, LANES, T), BF16),
                        pltpu.VMEM((N_SUB, 1, T), F32), pltpu.VMEM((N_SUB, 1, T), F32),
                        pltpu.VMEM((N_SUB * DIFF_VDIM, T), F32),
                        pltpu.VMEM((2, N_SUB, T, T), F32), pltpu.VMEM((2, N_SUB, T, T), BF16),
                        pltpu.VMEM((2, N_SUB, 1, T), F32)],
        compiler_params=_cparams(("parallel", "arbitrary")),
        name="diff_prompt",
    )(qt, kb, vt, bd, bp, lam_vecs, g_cols)


def _sample_main_kernel(pt_ref, q_ref, knew_ref, vnew_ref, *refs, lam_init):
    del pt_ref
    k_refs = refs[:PAGES_PER_STEP]
    v_refs = refs[PAGES_PER_STEP:2 * PAGES_PER_STEP]
    (tail_ref, bias0_ref, lam_ref, g_ref, o_ref, sel_ref,
     qb_ref, gacc_ref, m_ref, l_ref, acc_ref) = refs[2 * PAGES_PER_STEP:]
    j = pl.program_id(1)
    nstep = pl.num_programs(1)
    ppb = MOBA_BLOCK // PAGE_SIZE

    @pl.when(j == 0)
    def _():
        row = lax.broadcasted_iota(jnp.int32, (LANES, DIFF_W), 0)
        sub_of_lane = lax.broadcasted_iota(jnp.int32, (LANES, DIFF_W), 1) // HEAD_DIM
        qm = jnp.where(row == sub_of_lane, jnp.broadcast_to(q_ref[:, MOBA_W:], (LANES, DIFF_W)), 0.0)
        qb_ref[...] = (qm * ATTN_SCALE).T.astype(BF16)
        m_ref[...] = jnp.full(m_ref.shape, M_INIT, F32)
        l_ref[...] = jnp.zeros(l_ref.shape, F32)
        acc_ref[...] = jnp.zeros(acc_ref.shape, F32)

    far = tail_ref[0:1, :]
    ksum = None
    page_m, page_l, page_acc = [], [], []
    for pg in range(PAGES_PER_STEP):
        k_ref, v_ref = k_refs[pg], v_refs[pg]
        part = jnp.sum(k_ref[:, :MOBA_W], axis=0, keepdims=True)
        ksum = part if pg % ppb == 0 else ksum + part
        if pg % ppb == ppb - 1:
            blk = j * (PAGES_PER_STEP // ppb) + pg // ppb
            gacc_ref[pl.ds(blk, 1), :] = _mxu_round(ksum * (1.0 / MOBA_BLOCK)) * _mxu_round(q_ref[:, :MOBA_W])
        s = _dot(k_ref[:, MOBA_W:], qb_ref[...])
        tail_pg = pg - (PAGES_PER_STEP - ppb)
        if tail_pg >= 0:
            bias = jnp.where(j == nstep - 1, tail_ref[tail_pg * PAGE_SIZE:(tail_pg + 1) * PAGE_SIZE, :], far)
        else:
            bias = far
        s = s + bias
        m_pg = _colmax(s)
        p = jnp.exp(s - m_pg)
        page_m.append(m_pg)
        page_l.append(_fold8(p))
        pr = _mxu_round(p)
        vr = _mxu_round(v_ref[...])
        page_acc.append([_fold8(pr[:, r:r + 1] * vr[:, (r // 2) * DIFF_VDIM:(r // 2 + 1) * DIFF_VDIM])
                         for r in range(N_SUB)])
    m_old = m_ref[...]
    m_new = m_old
    for m_pg in page_m:
        m_new = jnp.maximum(m_new, m_pg)
    alpha = jnp.exp(m_old - m_new)
    weights = [jnp.exp(m_pg - m_new) for m_pg in page_m]
    l_ref[...] = alpha * l_ref[...] + sum(w * l for w, l in zip(weights, page_l))
    m_ref[...] = m_new
    for r in range(N_SUB):
        acc_ref[r] = alpha[:, r:r + 1] * acc_ref[r] + sum(w[:, r:r + 1] * acc[r]
                                                          for w, acc in zip(weights, page_acc))

    @pl.when(j == nstep - 1)
    def _():
        kd = jnp.broadcast_to(knew_ref[:, MOBA_W:], (SUBLANES, DIFF_W))
        s_self = _dot(kd, qb_ref[...])[0:1] + bias0_ref[...]
        m_old = m_ref[...]
        m_new = jnp.maximum(m_old, s_self)
        alpha = jnp.exp(m_old - m_new)
        p_self = jnp.exp(s_self - m_new)
        l = alpha * jnp.sum(l_ref[...], axis=0, keepdims=True) + p_self
        pr_self = _mxu_round(p_self)
        v_self = _mxu_round(vnew_ref[:, MOBA_W:])
        lam = _lambda(lam_ref, lam_init)
        for h in range(DIFF_HEADS):
            vcols = slice(h * DIFF_VDIM, (h + 1) * DIFF_VDIM)
            parts = []
            for c in range(2):
                r = 2 * h + c
                num = (alpha[:, r:r + 1] * jnp.sum(acc_ref[r], axis=0, keepdims=True)
                       + pr_self[:, r:r + 1] * v_self[:, vcols])
                parts.append(num / l[:, r:r + 1])
            oh = parts[0] - lam * parts[1]
            o_ref[:, vcols] = (oh * lax.rsqrt(jnp.mean(oh * oh, -1, keepdims=True) + LN_EPS)
                               * g_ref[...] * (1.0 - lam_init))
        head_row = lax.broadcasted_iota(jnp.int32, (MOBA_HEADS, MOBA_W), 0)
        head_of_lane = lax.broadcasted_iota(jnp.int32, (MOBA_HEADS, MOBA_W), 1) // HEAD_DIM
        seg = jnp.where(head_row == head_of_lane, 1.0, 0.0)
        gate = lax.dot_general(seg, gacc_ref[...], (((1,), (1,)), ((), ())), precision=HIGHEST,
                               preferred_element_type=F32)
        ids = lax.broadcasted_iota(jnp.int32, gate.shape, 1).astype(F32)
        out_lane = lax.broadcasted_iota(jnp.int32, (MOBA_HEADS, LANES), 1)
        picked = jnp.zeros((MOBA_HEADS, LANES), F32)
        for t in range(MOBA_TOPK):
            mx = jnp.max(gate, axis=-1, keepdims=True)
            idx = jnp.min(jnp.where(gate == mx, ids, float(gate.shape[-1])), axis=-1, keepdims=True)
            picked = jnp.where(out_lane == t, idx, picked)
            gate = jnp.where(ids == idx, NEG_INF, gate)
        sel_ref[...] = picked.astype(jnp.int32)


def _sample_main(page_table, q_s, k_s, v_s, cache_k, cache_v, layer, tab_d, lam_vecs, subln_g, lam_init):
    db, n_pages = page_table.shape
    past = n_pages * PAGE_SIZE
    assert n_pages % PAGES_PER_STEP == 0 and past // MOBA_BLOCK >= MOBA_TOPK
    nblk = past // MOBA_BLOCK
    nstep = n_pages // PAGES_PER_STEP
    tab_sub = jnp.repeat(tab_d, 2, axis=0)
    tail = _bias_lookup(tab_sub, MOBA_BLOCK - jnp.arange(MOBA_BLOCK)).T
    tail = jnp.pad(tail, ((0, 0), (0, LANES - N_SUB)))
    bias_self = jnp.pad(tab_sub[:, 0].reshape(1, N_SUB), ((0, 0), (0, LANES - N_SUB)))

    def page(pg, col):
        return lambda b, j, pt: (pt[b * n_pages + PAGES_PER_STEP * j + pg], layer, 0, col)

    tok = lambda b, j, pt: (b, 0, 0)
    const2 = lambda b, j, pt: (0, 0)
    grid_spec = pltpu.PrefetchScalarGridSpec(
        num_scalar_prefetch=1,
        grid=(db, nstep),
        in_specs=([pl.BlockSpec((None, 1, MIX_W), tok)] * 3
                  + [pl.BlockSpec((None, None, PAGE_SIZE, MIX_W), page(pg, 0)) for pg in range(PAGES_PER_STEP)]
                  + [pl.BlockSpec((None, None, PAGE_SIZE, DIFF_W), page(pg, 1)) for pg in range(PAGES_PER_STEP)]
                  + [pl.BlockSpec((MOBA_BLOCK, LANES), const2), pl.BlockSpec((1, LANES), const2),
                     pl.BlockSpec((4, HEAD_DIM), const2), pl.BlockSpec((1, DIFF_VDIM), const2)]),
        out_specs=[pl.BlockSpec((None, 1, DIFF_W), tok),
                   pl.BlockSpec((None, MOBA_HEADS, LANES), tok)],
        scratch_shapes=[pltpu.VMEM((DIFF_W, LANES), BF16), pltpu.VMEM((nblk, MOBA_W), F32),
                        pltpu.VMEM((1, LANES), F32), pltpu.VMEM((SUBLANES, LANES), F32),
                        pltpu.VMEM((N_SUB, SUBLANES, DIFF_VDIM), F32)],
    )
    o_d, sel = pl.pallas_call(
        functools.partial(_sample_main_kernel, lam_init=lam_init),
        grid_spec=grid_spec,
        out_shape=[jax.ShapeDtypeStruct((db, 1, DIFF_W), F32),
                   jax.ShapeDtypeStruct((db, MOBA_HEADS, LANES), jnp.int32)],
        compiler_params=_cparams(("parallel", "arbitrary")),
        name="sample_main",
    )(page_table.reshape(-1), q_s, k_s, v_s, *([cache_k] * PAGES_PER_STEP), *([cache_v] * PAGES_PER_STEP),
      tail, bias_self, lam_vecs, subln_g)
    return o_d.reshape(db, DIFF_W), sel[:, :, :MOBA_TOPK]


def _sample_moba_kernel(pt_ref, sel_ref, q_ref, knew_ref, vnew_ref, *refs):
    del pt_ref, sel_ref
    n_pg = MOBA_TOPK * (MOBA_BLOCK // PAGE_SIZE)
    k_refs = refs[:n_pg]
    v_refs = refs[n_pg:2 * n_pg]
    bias_refs = refs[2 * n_pg:2 * n_pg + MOBA_TOPK]
    bias0_ref, o_ref = refs[2 * n_pg + MOBA_TOPK:]
    h = pl.program_id(1)
    par = h % 2
    lane = lax.broadcasted_iota(jnp.int32, (1, LANES), 1)
    mine = (lane >= par * HEAD_DIM) & (lane < (par + 1) * HEAD_DIM)
    qm = jnp.where(mine, q_ref[...], 0.0) * ATTN_SCALE
    q_rep = jnp.broadcast_to(qm, (LANES, LANES)).T.astype(BF16)
    m = jnp.full((1, LANES), M_INIT, F32)
    l8 = jnp.zeros((SUBLANES, LANES), F32)
    acc = jnp.zeros((SUBLANES, LANES), F32)
    ppb = MOBA_BLOCK // PAGE_SIZE
    for pg in range(n_pg):
        brow = bias_refs[pg // ppb][:, (pg % ppb) * PAGE_SIZE:(pg % ppb + 1) * PAGE_SIZE]
        s = _dot(k_refs[pg][...], q_rep)
        s = s + jnp.broadcast_to(brow, (PAGE_SIZE, PAGE_SIZE)).T
        m_new = jnp.maximum(m, _colmax(s))
        p = jnp.exp(s - m_new)
        alpha = jnp.exp(m - m_new)
        l8 = alpha * l8 + _fold8(p)
        acc = alpha * acc + _fold8(_mxu_round(p) * _mxu_round(v_refs[pg][...]))
        m = m_new
    s_self = jnp.sum(_mxu_round(qm) * _mxu_round(knew_ref[...]), axis=-1, keepdims=True) + bias0_ref[:, 0:1]
    m_new = jnp.maximum(m, s_self)
    alpha = jnp.exp(m - m_new)
    p_self = jnp.exp(s_self - m_new)
    o = ((alpha * jnp.sum(acc, axis=0, keepdims=True) + _mxu_round(p_self) * _mxu_round(vnew_ref[...]))
         / (alpha * jnp.sum(l8, axis=0, keepdims=True) + p_self))
    o_ref[...] = jnp.where(par == 0, o[:, :HEAD_DIM], o[:, HEAD_DIM:])


def _sample_moba(page_table, sel, q_s, k_s, v_s, cache_k, cache_v, layer, tab_m):
    db, n_pages = page_table.shape
    past = n_pages * PAGE_SIZE
    ppb = MOBA_BLOCK // PAGE_SIZE
    n_pg = MOBA_TOPK * ppb
    bias_past = _bias_lookup(tab_m, past - jnp.arange(past)).reshape(MOBA_HEADS, 1, past)
    bias_self = jnp.broadcast_to(tab_m[:, 0:1, None], (MOBA_HEADS, 1, LANES))

    def sel_at(b, h, t, sl):
        return sl[(b * MOBA_HEADS + h) * MOBA_TOPK + t]

    def page(pg):
        return lambda b, h, pt, sl: (pt[b * n_pages + ppb * sel_at(b, h, pg // ppb, sl) + pg % ppb], layer, 0, h // 2)

    def bias_blk(t):
        return lambda b, h, pt, sl: (h, 0, sel_at(b, h, t, sl))

    tok = lambda b, h, pt, sl: (b, 0, h // 2)
    grid_spec = pltpu.PrefetchScalarGridSpec(
        num_scalar_prefetch=2,
        grid=(db, MOBA_HEADS),
        in_specs=([pl.BlockSpec((None, 1, LANES), tok)] * 3
                  + [pl.BlockSpec((None, None, PAGE_SIZE, LANES), page(pg)) for pg in range(n_pg)] * 2
                  + [pl.BlockSpec((None, 1, MOBA_BLOCK), bias_blk(t)) for t in range(MOBA_TOPK)]
                  + [pl.BlockSpec((None, 1, LANES), lambda b, h, pt, sl: (h, 0, 0))]),
        out_specs=pl.BlockSpec((None, None, 1, HEAD_DIM), lambda b, h, pt, sl: (b, h, 0, 0)),
    )
    o_m = pl.pallas_call(
        _sample_moba_kernel,
        grid_spec=grid_spec,
        out_shape=jax.ShapeDtypeStruct((db, MOBA_HEADS, 1, HEAD_DIM), F32),
        compiler_params=_cparams(("parallel", "arbitrary")),
        name="sample_moba",
    )(page_table.reshape(-1), sel.reshape(-1), q_s, k_s, v_s,
      *([cache_k] * n_pg), *([cache_v] * n_pg), *([bias_past] * MOBA_TOPK), bias_self)
    return o_m.reshape(db, MOBA_W)


def _post_attn_kernel(h_ref, mm_ref, md_ref, wo_ref, g_ref, b_ref, wr_ref, br_ref,
                      a_ref, eid_ref, gate_ref, *, alpha, transposed):
    if transposed:
        mm = mm_ref[...].astype(F32).T
        md = md_ref[...].astype(F32).T
    else:
        mm, md = mm_ref[...], md_ref[...]
    mix = _dot(mm, wo_ref[:MOBA_W, :]) + _dot(md, wo_ref[MOBA_W:, :])
    a = _layer_norm(alpha * h_ref[...] + mix, g_ref[...], b_ref[...])
    _store_tiles(a_ref, a)
    logits = _dot(a, wr_ref[...]) + br_ref[...]
    lane = lax.broadcasted_iota(jnp.int32, logits.shape, 1).astype(F32)
    none = float(LANES)
    gl = jnp.where(lane < N_GROUPS, logits, NEG_INF)
    gmax = jnp.max(gl, axis=-1, keepdims=True)
    g_gate = 1.0 / jnp.sum(jnp.exp(gl - gmax), axis=-1, keepdims=True)
    g_idx = jnp.min(jnp.where(gl == gmax, lane, none), axis=-1, keepdims=True)
    lo = N_GROUPS + EXPERTS_PER_GROUP * g_idx
    el = jnp.where(lane >= lo, jnp.where(lane < lo + EXPERTS_PER_GROUP, logits, NEG_INF), NEG_INF)
    m1 = jnp.max(el, axis=-1, keepdims=True)
    i1 = jnp.min(jnp.where(el == m1, lane, none), axis=-1, keepdims=True)
    el2 = jnp.where(lane == i1, NEG_INF, el)
    m2 = jnp.max(el2, axis=-1, keepdims=True)
    i2 = jnp.min(jnp.where(el2 == m2, lane, none), axis=-1, keepdims=True)
    r = jnp.exp(m2 - m1)
    w1 = g_gate / (1.0 + r)
    w2 = g_gate * r / (1.0 + r)
    eid_ref[...] = jnp.where(lane == 0.0, i1 - N_GROUPS, jnp.where(lane == 1.0, i2 - N_GROUPS, 0.0)).astype(jnp.int32)
    gate_ref[...] = jnp.where(lane == 0.0, w1, jnp.where(lane == 1.0, w2, 0.0))


def _post_attn(h, mix_m, mix_d, w_out, ln_g, ln_b, w_router, b_router, alpha, tm, transposed):
    m = h.shape[0]
    row = lambda i: (i, 0)
    const = lambda i: (0, 0)
    if transposed:
        assert tm == T
        mix_specs = [pl.BlockSpec((None, MOBA_W, T), lambda i: (i, 0, 0)),
                     pl.BlockSpec((None, DIFF_W, T), lambda i: (i, 0, 0))]
    else:
        mix_specs = [pl.BlockSpec((tm, MOBA_W), row), pl.BlockSpec((tm, DIFF_W), row)]
    return pl.pallas_call(
        functools.partial(_post_attn_kernel, alpha=alpha, transposed=transposed),
        grid=(m // tm,),
        in_specs=[pl.BlockSpec((tm, D_MODEL), row)] + mix_specs + [
            pl.BlockSpec((MIX_W, D_MODEL), const),
            pl.BlockSpec((1, D_MODEL), const), pl.BlockSpec((1, D_MODEL), const),
            pl.BlockSpec((D_MODEL, LANES), const), pl.BlockSpec((1, LANES), const)],
        out_specs=[pl.BlockSpec((tm * N_CHUNK, LANES), row), pl.BlockSpec((tm, LANES), row),
                   pl.BlockSpec((tm, LANES), row)],
        out_shape=[jax.ShapeDtypeStruct((m * N_CHUNK, LANES), F32), jax.ShapeDtypeStruct((m, LANES), jnp.int32),
                   jax.ShapeDtypeStruct((m, LANES), F32)],
        compiler_params=_cparams(("parallel",)),
        name="post_attn",
    )(h, mix_m, mix_d, w_out, ln_g, ln_b, w_router, b_router)


def _moe_kernel(code_ref, blk_e_ref, blk_src_ref, blk_n_ref, nblk_ref, a_hbm, w1_ref, w3_ref, w2_ref, y_hbm,
                xbuf, ybuf, w1b, w3b, w2b, gsem, ssem, *, rb, n_tok):
    i = pl.program_id(0)
    n_used = nblk_ref[0]

    def tile(t):
        start = t * N_CHUNK
        return pl.ds(start if isinstance(t, int) else pl.multiple_of(start, N_CHUNK), N_CHUNK)

    def gather_row(base, r, slot):
        tok = code_ref[base + r] >> 1
        return pltpu.make_async_copy(a_hbm.at[tile(tok)], xbuf.at[slot, tile(r)], gsem.at[slot])

    def scatter_row(base, r, n_valid, slot):
        code = code_ref[base + r]
        dst = jnp.where(r < n_valid, (code >> 1) + (code & 1) * n_tok, 2 * n_tok + slot * rb + r)
        return pltpu.make_async_copy(ybuf.at[slot, tile(r)], y_hbm.at[tile(dst)], ssem.at[slot])

    def for_rows(fn):
        def body(r, carry):
            fn(r)
            return carry
        lax.fori_loop(0, rb, body, 0, unroll=SUBLANES)

    def wait_gather(slot):
        pltpu.make_async_copy(xbuf.at[slot], xbuf.at[slot], gsem.at[slot]).wait()

    def wait_scatter(slot):
        pltpu.make_async_copy(ybuf.at[slot], ybuf.at[slot], ssem.at[slot]).wait()

    def step(slot):
        other = 1 - slot

        def first_step():
            base0 = blk_src_ref[0]
            for_rows(lambda r: gather_row(base0, r, 0).start())
            ybuf[1] = jnp.zeros(ybuf.shape[1:], F32)
            for half in range(2):
                for_rows(lambda r, half=half: pltpu.make_async_copy(
                    ybuf.at[1, tile(r)], y_hbm.at[tile(2 * n_tok + half * rb + r)], ssem.at[1]).start())
                wait_scatter(1)

        if slot == 0:
            pl.when(i == 0)(first_step)

        e = blk_e_ref[i]

        @pl.when((i == 0) | (e != blk_e_ref[jnp.maximum(i - 1, 0)]))
        def _():
            w1b[...] = w1_ref[...].astype(BF16)
            w3b[...] = w3_ref[...].astype(BF16)
            w2b[...] = w2_ref[...].astype(BF16)

        wait_gather(slot)

        @pl.when(i >= 1)
        def _():
            wait_scatter(slot)

        nxt_base = blk_src_ref[jnp.minimum(i + 1, pl.num_programs(0) - 1)]
        prv = jnp.maximum(i - 1, 0)
        prv_base = blk_src_ref[prv]
        prv_valid = jnp.where(i >= 1, blk_n_ref[prv], 0)

        for_rows(lambda r: gather_row(nxt_base, r, other).start())
        x = _load_tiles(xbuf.at[slot]).astype(BF16)
        h1 = jnp.dot(x, w1b[...], preferred_element_type=F32)
        h3 = jnp.dot(x, w3b[...], preferred_element_type=F32)
        hidden = (h1 * jax.nn.sigmoid(h1) * h3).astype(BF16)
        for r in range(rb):
            scatter_row(prv_base, r, prv_valid, other).start(priority=1)
        y = jnp.dot(hidden, w2b[...], preferred_element_type=F32)
        _store_tiles(ybuf.at[slot], y)

        @pl.when(i == n_used - 1)
        def _():
            base = blk_src_ref[i]
            n_valid = blk_n_ref[i]
            for_rows(lambda r: scatter_row(base, r, n_valid, slot).start())
            wait_scatter(slot)
            wait_scatter(other)
            wait_gather(other)

    for parity in range(2):
        pl.when((i < n_used) & (i % 2 == parity))(functools.partial(step, parity))


def _route_tables(eid, rb):
    n = eid.shape[0]
    m = n * 2
    e_flat = eid.reshape(m)
    keys = lax.sort(e_flat * m + jnp.arange(m, dtype=jnp.int32))
    code = keys % m
    experts = jnp.arange(N_EXPERTS, dtype=jnp.int32)
    counts = jnp.sum((e_flat[:, None] == experts[None, :]).astype(jnp.int32), axis=0)
    padded = (counts + rb - 1) // rb * rb
    start = jnp.cumsum(counts) - counts
    p_end = jnp.cumsum(padded)
    p_start = p_end - padded
    n_blk = -(-(m + N_EXPERTS * (rb - 1)) // rb)
    blk_row = jnp.arange(n_blk, dtype=jnp.int32) * rb
    blk_e = jnp.minimum(jnp.sum((p_end[None, :] <= blk_row[:, None]).astype(jnp.int32), axis=1), N_EXPERTS - 1)
    onehot = (blk_e[:, None] == experts[None, :]).astype(jnp.int32)
    pick = lambda table: jnp.sum(onehot * table[None, :], axis=1)
    off = blk_row - pick(p_start)
    blk_src = jnp.clip(pick(start) + off, 0, m - 1).astype(jnp.int32)
    blk_n = jnp.clip(pick(counts) - off, 0, rb).astype(jnp.int32)
    n_used = (p_end[-1:] // rb).astype(jnp.int32)
    code = jnp.pad(code.astype(jnp.int32), (0, rb))
    return code, blk_e.astype(jnp.int32), blk_src, blk_n, n_used, n_blk


def _moe(a_tiles, eid, w1, w3, w2, rb):
    n = a_tiles.shape[0] // N_CHUNK
    code, blk_e, blk_src, blk_n, n_used, n_blk = _route_tables(eid, rb)
    wmap = lambda i, code, be, bs, bn, nu: (be[i], 0, 0)
    grid_spec = pltpu.PrefetchScalarGridSpec(
        num_scalar_prefetch=5,
        grid=(n_blk,),
        in_specs=[pl.BlockSpec(memory_space=pl.ANY),
                  pl.BlockSpec((None, D_MODEL, D_EXPERT), wmap),
                  pl.BlockSpec((None, D_MODEL, D_EXPERT), wmap),
                  pl.BlockSpec((None, D_EXPERT, D_MODEL), wmap)],
        out_specs=pl.BlockSpec(memory_space=pl.ANY),
        scratch_shapes=[pltpu.VMEM((2, rb * N_CHUNK, LANES), F32), pltpu.VMEM((2, rb * N_CHUNK, LANES), F32),
                        pltpu.VMEM((D_MODEL, D_EXPERT), BF16), pltpu.VMEM((D_MODEL, D_EXPERT), BF16),
                        pltpu.VMEM((D_EXPERT, D_MODEL), BF16),
                        pltpu.SemaphoreType.DMA((2,)), pltpu.SemaphoreType.DMA((2,))],
    )
    return pl.pallas_call(
        functools.partial(_moe_kernel, rb=rb, n_tok=n),
        grid_spec=grid_spec,
        out_shape=jax.ShapeDtypeStruct(((2 * n + 2 * rb) * N_CHUNK, LANES), F32),
        compiler_params=_cparams(("arbitrary",)),
        name="moe",
    )(code, blk_e, blk_src, blk_n, n_used, a_tiles, w1, w3, w2)


def _finish_kernel(a_ref, y0_ref, y1_ref, gate_ref, p_ref, g_ref, b_ref, wpg_ref, bpg_ref, wple_ref, o_ref,
                   *, alpha):
    gates = gate_ref[...]
    moe = _load_tiles(y0_ref) * gates[:, 0:1] + _load_tiles(y1_ref) * gates[:, 1:2]
    c = _layer_norm(alpha * _load_tiles(a_ref) + moe, g_ref[...], b_ref[...])
    gate = jax.nn.sigmoid(_dot(c, wpg_ref[...]) + bpg_ref[...])
    o_ref[...] = c + gate * _dot(p_ref[...], wple_ref[...])


def _finish(a, y, gates, p, ln_g, ln_b, w_pg, b_pg, w_ple, alpha, tm):
    m = a.shape[0] // N_CHUNK
    assert m % tm == 0
    row = lambda i: (i, 0)
    const = lambda i: (0, 0)
    tiles = (tm * N_CHUNK, LANES)
    return pl.pallas_call(
        functools.partial(_finish_kernel, alpha=alpha),
        grid=(m // tm,),
        in_specs=[pl.BlockSpec(tiles, row), pl.BlockSpec(tiles, row),
                  pl.BlockSpec(tiles, lambda i: (m // tm + i, 0)),
                  pl.BlockSpec((tm, LANES), row), pl.BlockSpec((tm, D_PLE), row),
                  pl.BlockSpec((1, D_MODEL), const), pl.BlockSpec((1, D_MODEL), const),
                  pl.BlockSpec((D_MODEL, D_MODEL), const), pl.BlockSpec((1, D_MODEL), const),
                  pl.BlockSpec((D_PLE, D_MODEL), const)],
        out_specs=pl.BlockSpec((tm, D_MODEL), row),
        out_shape=jax.ShapeDtypeStruct((m, D_MODEL), F32),
        compiler_params=_cparams(("parallel",)),
        name="finish",
    )(a, y, y, gates, p, ln_g, ln_b, w_pg, b_pg, w_ple)


def _pack_router(w_group, b_group, w_er, b_er):
    w = jnp.concatenate([w_group, jnp.transpose(w_er, (1, 0, 2)).reshape(D_MODEL, N_EXPERTS)], axis=1)
    b = jnp.concatenate([b_group, b_er.reshape(N_EXPERTS)])
    pad = LANES - w.shape[1]
    return jnp.pad(w, ((0, 0), (0, pad))), jnp.pad(b, (0, pad)).reshape(1, LANES)


def kernel(x_prompt, x_sample, cache_k, cache_v, page_table, p_prompt, p_sample, rel_bias, w_in, w_out, lam_q1, lam_k1, lam_q2, lam_k2, subln_g, ln1_g, ln1_b, w_group, b_group, w_erouter, b_erouter, w1, w3, w2, ln2_g, ln2_b, w_ple, w_pg, b_pg):
    depth = w_in.shape[0]
    b, s, _ = x_prompt.shape
    db, ds, _ = x_sample.shape
    assert ds == 1 and s % T == 0
    alpha = (2 * depth) ** 0.25
    tab_m = rel_bias[:, :MOBA_HEADS].T
    tab_d = rel_bias[:, MOBA_HEADS:].T
    hp = x_prompt.reshape(b * s, D_MODEL)
    hs = x_sample.reshape(db, D_MODEL)
    kp_rows, vp_rows, ks_rows, vs_rows = [], [], [], []
    for i in range(depth):
        lam_init = 0.8 - 0.6 * math.exp(-0.3 * i)
        lam_vecs = jnp.stack([lam_q1[i], lam_k1[i], lam_q2[i], lam_k2[i]]).astype(F32)
        g_sub = subln_g[i].reshape(1, DIFF_VDIM)
        w_router, b_router = _pack_router(w_group[i], b_group[i], w_erouter[i], b_erouter[i])
        ln1 = (ln1_g[i].reshape(1, D_MODEL), ln1_b[i].reshape(1, D_MODEL))
        ln2 = (ln2_g[i].reshape(1, D_MODEL), ln2_b[i].reshape(1, D_MODEL))
        b_pg_i = b_pg[i].reshape(1, D_MODEL)

        w_in_b, w_out_b = w_in[i].astype(BF16), w_out[i].astype(BF16)
        w_pg_b, w_ple_b, w_router_b = w_pg[i].astype(BF16), w_ple[i].astype(BF16), w_router.astype(BF16)

        k_p, v_p, qt, kb, vt, kmean = _qkv_prompt(hp, w_in_b)
        mix_m = _moba_prompt(qt, kb, vt, kmean, tab_m, b, s)
        mix_d = _diff_prompt(qt, kb, vt, tab_d, lam_vecs, g_sub, lam_init, b, s)
        a_p, eid_p, gate_p = _post_attn(hp, mix_m, mix_d, w_out_b, *ln1, w_router_b, b_router, alpha, T, True)
        y_p = _moe(a_p, eid_p[:, :2], w1[i], w3[i], w2[i], 256)
        hp = _finish(a_p, y_p, gate_p, p_prompt[i].reshape(b * s, D_PLE), *ln2, w_pg_b, b_pg_i, w_ple_b,
                     alpha, 256)

        qkv_s = _mm_small(hs, w_in_b, MIX_W)
        q_s = qkv_s[:, :MIX_W].reshape(db, 1, MIX_W)
        k_s = qkv_s[:, MIX_W:2 * MIX_W].reshape(db, 1, MIX_W)
        v_s = qkv_s[:, 2 * MIX_W:].reshape(db, 1, MIX_W)
        o_d, sel = _sample_main(page_table, q_s, k_s, v_s, cache_k, cache_v, i, tab_d, lam_vecs, g_sub, lam_init)
        o_m = _sample_moba(page_table, sel, q_s, k_s, v_s, cache_k, cache_v, i, tab_m)
        a_s, eid_s, gate_s = _post_attn(hs, o_m, o_d, w_out_b, *ln1, w_router_b, b_router, alpha, db, False)
        y_s = _moe(a_s, eid_s[:, :2], w1[i], w3[i], w2[i], 64)
        hs = _finish(a_s, y_s, gate_s, p_sample[i].reshape(db, D_PLE), *ln2, w_pg_b, b_pg_i, w_ple_b, alpha, db)

        kp_rows.append(k_p.reshape(b, s, MIX_W))
        vp_rows.append(v_p.reshape(b, s, MIX_W))
        ks_rows.append(k_s.reshape(db, ds, MIX_W))
        vs_rows.append(v_s.reshape(db, ds, MIX_W))
    return (hp.reshape(b, s, D_MODEL), hs.reshape(db, ds, D_MODEL),
            jnp.stack(kp_rows, axis=1), jnp.stack(vp_rows, axis=1),
            jnp.stack(ks_rows, axis=1), jnp.stack(vs_rows, axis=1))
```

```python
import functools
import math

import jax
import jax.numpy as jnp
from jax import lax
from jax.experimental import pallas as pl
from jax.experimental.pallas import tpu as pltpu

D_MODEL = 1024
PAGE_SIZE = 128
HEAD_DIM = 64
MOBA_HEADS = 8
MOBA_W = MOBA_HEADS * HEAD_DIM
DIFF_HEADS = 4
DIFF_VDIM = 2 * HEAD_DIM
DIFF_W = DIFF_HEADS * DIFF_VDIM
N_SUB = 2 * DIFF_HEADS
MIX_W = MOBA_W + DIFF_W
MOBA_BLOCK = 256
MOBA_TOPK = 3
N_BUCKETS = 32
REL_MAX_DIST = 128
N_GROUPS = 4
EXPERTS_PER_GROUP = 8
N_EXPERTS = N_GROUPS * EXPERTS_PER_GROUP
D_EXPERT = D_MODEL // 2
D_PLE = 256
LN_EPS = 1e-5
ATTN_SCALE = HEAD_DIM ** -0.5
LANES = 128
SUBLANES = 8
N_CHUNK = D_MODEL // LANES
PAGES_PER_STEP = 16
F32 = jnp.float32
BF16 = jnp.bfloat16
HIGHEST = lax.Precision.HIGHEST
NEG_INF = float("-inf")
M_INIT = -1e30
VMEM_LIMIT = 48 * 1024 * 1024
T = MOBA_BLOCK


def _cparams(sem):
    return pltpu.CompilerParams(dimension_semantics=sem, vmem_limit_bytes=VMEM_LIMIT)


def _dot(a, b):
    return jnp.dot(a.astype(BF16), b.astype(BF16), preferred_element_type=F32)


def _mxu_round(x):
    return x.astype(BF16).astype(F32)


def _rel_bucket(rel):
    n = jnp.maximum(rel, 0)
    max_exact = N_BUCKETS // 2
    large = max_exact + (jnp.log(jnp.maximum(n, 1).astype(F32) / max_exact)
                         / math.log(REL_MAX_DIST / max_exact) * (N_BUCKETS - max_exact)).astype(jnp.int32)
    large = jnp.minimum(large, N_BUCKETS - 1)
    return jnp.where(n < max_exact, n, large)


def _bias_lookup(tab, rel):
    onehot = (_rel_bucket(rel)[..., None] == jnp.arange(N_BUCKETS)).astype(F32)
    return jnp.einsum('hb,...b->h...', tab, onehot, precision=HIGHEST)


def _layer_norm(x, g, b):
    xc = x - jnp.mean(x, -1, keepdims=True)
    var = jnp.mean(xc * xc, -1, keepdims=True)
    return xc * lax.rsqrt(var + LN_EPS) * g + b


def _fold8(x):
    return jnp.sum(x.reshape(x.shape[0] // SUBLANES, SUBLANES, x.shape[1]), axis=0)


def _store_tiles(ref, x):
    rows = x.shape[0]
    for j in range(N_CHUNK):
        ref[pl.ds(j, rows, stride=N_CHUNK), :] = x[:, j * LANES:(j + 1) * LANES]


def _load_tiles(ref):
    rows = ref.shape[0] // N_CHUNK
    return jnp.concatenate([ref[pl.ds(j, rows, stride=N_CHUNK), :] for j in range(N_CHUNK)], axis=1)


def _colmax(x):
    part = jnp.max(x.reshape(x.shape[0] // SUBLANES, SUBLANES, x.shape[1]), axis=0)
    return jnp.max(part, axis=0, keepdims=True)


def _qkv_prompt_kernel(x_ref, w_ref, k_ref, v_ref, qt_ref, kb_ref, vt_ref, kmean_ref):
    x = x_ref[...].astype(BF16)
    q = jnp.dot(x, w_ref[:, :MIX_W], preferred_element_type=F32)
    qt_ref[...] = q.T.astype(BF16)
    k = jnp.dot(x, w_ref[:, MIX_W:2 * MIX_W], preferred_element_type=F32)
    k_ref[...] = k
    kb_ref[...] = k.astype(BF16)
    kmean_ref[...] = jnp.sum(k[:, :MOBA_W], axis=0, keepdims=True) * (1.0 / MOBA_BLOCK)
    v = jnp.dot(x, w_ref[:, 2 * MIX_W:], preferred_element_type=F32)
    v_ref[...] = v
    vt_ref[...] = v.T.astype(BF16)


def _qkv_prompt(x, w_bf16):
    m = x.shape[0]
    nt = m // T
    row = lambda i: (i, 0)
    tile3 = lambda i: (i, 0, 0)
    return pl.pallas_call(
        _qkv_prompt_kernel,
        grid=(nt,),
        in_specs=[pl.BlockSpec((T, D_MODEL), row),
                  pl.BlockSpec((D_MODEL, 3 * MIX_W), lambda i: (0, 0))],
        out_specs=[pl.BlockSpec((T, MIX_W), row), pl.BlockSpec((T, MIX_W), row),
                   pl.BlockSpec((None, MIX_W, T), tile3), pl.BlockSpec((T, MIX_W), row),
                   pl.BlockSpec((None, MIX_W, T), tile3),
                   pl.BlockSpec((None, 1, MOBA_W), tile3)],
        out_shape=[jax.ShapeDtypeStruct((m, MIX_W), F32), jax.ShapeDtypeStruct((m, MIX_W), F32),
                   jax.ShapeDtypeStruct((nt, MIX_W, T), BF16), jax.ShapeDtypeStruct((m, MIX_W), BF16),
                   jax.ShapeDtypeStruct((nt, MIX_W, T), BF16),
                   jax.ShapeDtypeStruct((nt, 1, MOBA_W), F32)],
        compiler_params=_cparams(("parallel",)),
        name="qkv_prompt",
    )(x, w_bf16)


def _mm_small_kernel(x_ref, w_ref, o_ref):
    o_ref[...] = _dot(x_ref[...], w_ref[...])


def _mm_small(x, w, tn):
    m, k = x.shape
    n = w.shape[1]
    return pl.pallas_call(
        _mm_small_kernel,
        grid=(n // tn,),
        in_specs=[pl.BlockSpec((m, k), lambda j: (0, 0)), pl.BlockSpec((k, tn), lambda j: (0, j))],
        out_specs=pl.BlockSpec((m, tn), lambda j: (0, j)),
        out_shape=jax.ShapeDtypeStruct((m, n), F32),
        compiler_params=_cparams(("parallel",)),
        name="mm_small",
    )(x, w)


def _stream_blocks(i, n_streams, scores, values, keep_row, bias_d, bias_p,
                   m_ref, l_ref, acc_ref, s_ref, p_ref, a_ref, dv):
    m_ref[...] = jnp.full(m_ref.shape, M_INIT, F32)
    l_ref[...] = jnp.zeros(l_ref.shape, F32)
    acc_ref[...] = jnp.zeros(acc_ref.shape, F32)

    def block(j, finish_scores, shift_of, gated, stage):
        col_max = []
        for r in range(n_streams):
            s = finish_scores(r, j, scores(r, j))
            s_ref[stage, r] = s
            col_max.append(_colmax(s))
        alphas = []
        for r in range(n_streams):
            shift = shift_of(r)
            m_old = m_ref[r]
            tile_max = col_max[r] + shift
            keep = keep_row(r, j) if gated else None
            if keep is not None:
                tile_max = jnp.where(keep > 0.5, tile_max, NEG_INF)
            m_new = jnp.maximum(m_old, tile_max)
            offset = m_new - shift
            if keep is not None:
                offset = jnp.where(keep > 0.5, offset, float("inf"))
            p = jnp.exp(s_ref[stage, r] - offset)
            alpha = jnp.exp(m_old - m_new)
            p_ref[stage, r] = p.astype(p_ref.dtype)
            m_ref[r] = m_new
            alphas.append(alpha)
        ones = jnp.ones((2 * SUBLANES, T), p_ref.dtype)
        for r in range(n_streams):
            rows = slice(r * dv, (r + 1) * dv)
            out = jnp.dot(jnp.concatenate([values(r, j), ones], axis=0), p_ref[stage, r],
                          preferred_element_type=F32)
            acc_ref[rows, :] = alphas[r] * acc_ref[rows, :] + out[:dv]
            l_ref[r] = alphas[r] * l_ref[r] + out[dv:dv + 1]

    n_far = jnp.maximum(i - 1, 0)
    last_far = jnp.maximum(n_far - 1, 0)

    def qk(j, slot):
        for r in range(n_streams):
            s_ref[slot, r] = scores(r, j)

    def softmax(t, slot):
        live = jnp.where(t < n_far, 0.0, NEG_INF).astype(F32).reshape(1, 1)
        for r in range(n_streams):
            shift = bias_p(r)[0:1, 0:1] + live
            s = s_ref[slot, r]
            m_old = m_ref[r]
            tile_max = _colmax(s) + shift
            keep = keep_row(r, t)
            if keep is not None:
                tile_max = jnp.where(keep > 0.5, tile_max, NEG_INF)
            m_new = jnp.maximum(m_old, tile_max)
            offset = m_new - shift
            if keep is not None:
                offset = jnp.where(keep > 0.5, offset, float("inf"))
            p_ref[slot, r] = jnp.exp(s - offset).astype(p_ref.dtype)
            a_ref[slot, r] = jnp.exp(m_old - m_new)
            m_ref[r] = m_new

    def pv(j, slot):
        ones = jnp.ones((2 * SUBLANES, T), p_ref.dtype)
        for r in range(n_streams):
            rows = slice(r * dv, (r + 1) * dv)
            out = jnp.dot(jnp.concatenate([values(r, j), ones], axis=0), p_ref[slot, r],
                          preferred_element_type=F32)
            alpha = a_ref[slot, r]
            acc_ref[rows, :] = alpha * acc_ref[rows, :] + out[:dv]
            l_ref[r] = alpha * l_ref[r] + out[dv:dv + 1]

    p_ref[1] = jnp.zeros(p_ref.shape[1:], p_ref.dtype)
    a_ref[1] = jnp.ones(a_ref.shape[1:], F32)
    qk(0, 0)

    def far_pair(u, carry):
        for slot in range(2):
            t = 2 * u + slot
            qk(jnp.minimum(t + 1, last_far), 1 - slot)
            softmax(t, slot)
            pv(jnp.maximum(t - 1, 0), 1 - slot)
        return carry

    n_pairs = (n_far + 1) // 2
    lax.fori_loop(0, n_pairs, far_pair, 0)
    pv(jnp.maximum(2 * n_pairs - 1, 0), 1)

    prev_shift = jnp.where(i >= 1, 0.0, NEG_INF).astype(F32).reshape(1, 1)
    block(jnp.maximum(i - 1, 0), lambda r, j, s: s + bias_p(r)[...], lambda r: prev_shift, True, 0)

    key = lax.broadcasted_iota(jnp.int32, (T, T), 0)
    qry = lax.broadcasted_iota(jnp.int32, (T, T), 1)
    zero = jnp.zeros((1, 1), F32)
    block(i, lambda r, j, s: jnp.where(qry >= key, s + bias_d(r)[...], NEG_INF), lambda r: zero, False, 1)


def _masked_half(qt_slab, half):
    row = lax.broadcasted_iota(jnp.int32, qt_slab.shape, 0)
    mine = (row >= half * HEAD_DIM) & (row < (half + 1) * HEAD_DIM)
    return jnp.where(mine, qt_slab, jnp.zeros_like(qt_slab))


def _top_blocks_t(gate_t, n_valid):
    nb = gate_t.shape[0]
    ids = lax.broadcasted_iota(jnp.int32, gate_t.shape, 0).astype(F32)
    g = jnp.where(ids < n_valid.astype(F32), gate_t, NEG_INF)
    sel = jnp.zeros(gate_t.shape, F32)
    for _ in range(MOBA_TOPK):
        mx = jnp.max(g, axis=0, keepdims=True)
        cand = jnp.where(g == mx, jnp.where(mx > NEG_INF, ids, float(nb)), float(nb))
        pick = ids == jnp.min(cand, axis=0, keepdims=True)
        sel = jnp.where(pick, 1.0, sel)
        g = jnp.where(pick, NEG_INF, g)
    return sel


def _moba_prompt_kernel(qt_ref, k_ref, vt_ref, kmean_ref, bd_ref, bp_ref, o_ref,
                        qs_ref, sel_ref, m_ref, l_ref, acc_ref, s_ref, p_ref, a_ref):
    i = pl.program_id(1)
    for h in range(MOBA_HEADS):
        slab = slice((h // 2) * LANES, (h // 2 + 1) * LANES)
        qh = _masked_half(qt_ref[slab, :], h % 2)
        gate_t = jnp.dot(kmean_ref[:, slab].astype(BF16), qh, preferred_element_type=F32)
        sel_ref[h] = _top_blocks_t(gate_t, i)
        qs_ref[h] = qh * ATTN_SCALE

    def scores(h, j):
        rows = pl.ds(pl.multiple_of(j * T, T), T)
        slab = slice((h // 2) * LANES, (h // 2 + 1) * LANES)
        return jnp.dot(k_ref[rows, slab], qs_ref[h], preferred_element_type=F32)

    _stream_blocks(i, MOBA_HEADS, scores,
                   lambda h, j: vt_ref[j, h * HEAD_DIM:(h + 1) * HEAD_DIM, :],
                   lambda h, j: sel_ref[h, pl.ds(j, 1), :],
                   lambda h: bd_ref.at[h], lambda h: bp_ref.at[h], m_ref, l_ref, acc_ref, s_ref, p_ref, a_ref,
                   HEAD_DIM)
    for h in range(MOBA_HEADS):
        rows = slice(h * HEAD_DIM, (h + 1) * HEAD_DIM)
        o_ref[rows, :] = (acc_ref[rows, :] / l_ref[h]).astype(o_ref.dtype)


def _lambda(lam_ref, lam_init):
    lv = lam_ref[...]
    s1 = jnp.sum(lv[0:1] * lv[1:2], axis=-1, keepdims=True)
    s2 = jnp.sum(lv[2:3] * lv[3:4], axis=-1, keepdims=True)
    return jnp.exp(s1) - jnp.exp(s2) + lam_init


def _diff_prompt_kernel(qt_ref, k_ref, vt_ref, bd_ref, bp_ref, lam_ref, g_ref, o_ref,
                        qs_ref, m_ref, l_ref, acc_ref, s_ref, p_ref, a_ref, *, lam_init):
    i = pl.program_id(1)
    for r in range(N_SUB):
        slab = slice((r // 2) * LANES, (r // 2 + 1) * LANES)
        qs_ref[r] = _masked_half(qt_ref[slab, :], r % 2) * ATTN_SCALE

    def scores(r, j):
        rows = pl.ds(pl.multiple_of(j * T, T), T)
        slab = slice((r // 2) * LANES, (r // 2 + 1) * LANES)
        return jnp.dot(k_ref[rows, slab], qs_ref[r], preferred_element_type=F32)

    _stream_blocks(i, N_SUB, scores,
                   lambda r, j: vt_ref[j, (r // 2) * DIFF_VDIM:(r // 2 + 1) * DIFF_VDIM, :],
                   lambda r, j: None,
                   lambda r: bd_ref.at[r // 2], lambda r: bp_ref.at[r // 2], m_ref, l_ref, acc_ref, s_ref, p_ref, a_ref,
                   DIFF_VDIM)
    lam = _lambda(lam_ref, lam_init)
    for h in range(DIFF_HEADS):
        parts = []
        for c in range(2):
            r = 2 * h + c
            parts.append(acc_ref[r * DIFF_VDIM:(r + 1) * DIFF_VDIM, :] / l_ref[r])
        o = parts[0] - lam * parts[1]
        o = o * lax.rsqrt(jnp.mean(o * o, axis=0, keepdims=True) + LN_EPS) * g_ref[...] * (1.0 - lam_init)
        o_ref[h * DIFF_VDIM:(h + 1) * DIFF_VDIM, :] = o.astype(o_ref.dtype)


def _prompt_bias_tiles(tab):
    key = jnp.arange(T)[:, None]
    qry = jnp.arange(T)[None, :]
    return _bias_lookup(tab, qry - key), _bias_lookup(tab, qry - key + T)


def _moba_prompt(qt, kb, vt, kmean, tab_m, b, s):
    nq = s // T
    bd, bp = _prompt_bias_tiles(tab_m)
    const3 = lambda bi, i: (0, 0, 0)
    return pl.pallas_call(
        _moba_prompt_kernel,
        grid=(b, nq),
        in_specs=[pl.BlockSpec((None, MOBA_W, T), lambda bi, i: (bi * nq + i, 0, 0)),
                  pl.BlockSpec((s, MOBA_W), lambda bi, i: (bi, 0)),
                  pl.BlockSpec((nq, MOBA_W, T), lambda bi, i: (bi, 0, 0)),
                  pl.BlockSpec((None, nq, MOBA_W), lambda bi, i: (bi, 0, 0)),
                  pl.BlockSpec((MOBA_HEADS, T, T), const3),
                  pl.BlockSpec((MOBA_HEADS, T, T), const3)],
        out_specs=pl.BlockSpec((None, MOBA_W, T), lambda bi, i: (bi * nq + i, 0, 0)),
        out_shape=jax.ShapeDtypeStruct((b * nq, MOBA_W, T), BF16),
        scratch_shapes=[pltpu.VMEM((MOBA_HEADS, LANES, T), BF16), pltpu.VMEM((MOBA_HEADS, nq, T), F32),
                        pltpu.VMEM((MOBA_HEADS, 1, T), F32), pltpu.VMEM((MOBA_HEADS, 1, T), F32),
                        pltpu.VMEM((MOBA_W, T), F32),
                        pltpu.VMEM((2, MOBA_HEADS, T, T), F32), pltpu.VMEM((2, MOBA_HEADS, T, T), BF16),
                        pltpu.VMEM((2, MOBA_HEADS, 1, T), F32)],
        compiler_params=_cparams(("parallel", "arbitrary")),
        name="moba_prompt",
    )(qt, kb, vt, kmean.reshape(b, nq, MOBA_W), bd, bp)


def _diff_prompt(qt, kb, vt, tab_d, lam_vecs, subln_g, lam_init, b, s):
    nq = s // T
    bd, bp = _prompt_bias_tiles(tab_d)
    g_cols = jnp.broadcast_to(subln_g.reshape(DIFF_VDIM, 1), (DIFF_VDIM, T))
    const3 = lambda bi, i: (0, 0, 0)
    const2 = lambda bi, i: (0, 0)
    return pl.pallas_call(
        functools.partial(_diff_prompt_kernel, lam_init=lam_init),
        grid=(b, nq),
        in_specs=[pl.BlockSpec((None, DIFF_W, T), lambda bi, i: (bi * nq + i, 1, 0)),
                  pl.BlockSpec((s, DIFF_W), lambda bi, i: (bi, 1)),
                  pl.BlockSpec((nq, DIFF_W, T), lambda bi, i: (bi, 1, 0)),
                  pl.BlockSpec((DIFF_HEADS, T, T), const3),
                  pl.BlockSpec((DIFF_HEADS, T, T), const3),
                  pl.BlockSpec((4, HEAD_DIM), const2),
                  pl.BlockSpec((DIFF_VDIM, T), const2)],
        out_specs=pl.BlockSpec((None, DIFF_W, T), lambda bi, i: (bi * nq + i, 0, 0)),
        out_shape=jax.ShapeDtypeStruct((b * nq, DIFF_W, T), BF16),
        scratch_shapes=[pltpu.VMEM((N_SUB, LANES, T), BF16),
                        pltpu.VMEM((N_SUB, 1, T), F32), pltpu.VMEM((N_SUB, 1, T), F32),
                        pltpu.VMEM((N_SUB * DIFF_VDIM, T), F32),
                        pltpu.VMEM((2, N_SUB, T, T), F32), pltpu.VMEM((2, N_SUB, T, T), BF16),
                        pltpu.VMEM((2, N_SUB, 1, T), F32)],
        compiler_params=_cparams(("parallel", "arbitrary")),
        name="diff_prompt",
    )(qt, kb, vt, bd, bp, lam_vecs, g_cols)


def _sample_main_kernel(pt_ref, q_ref, knew_ref, vnew_ref, *refs, lam_init):
    del pt_ref
    k_refs = refs[:PAGES_PER_STEP]
    v_refs = refs[PAGES_PER_STEP:2 * PAGES_PER_STEP]
    (tail_ref, bias0_ref, lam_ref, g_ref, o_ref, sel_ref,
     qb_ref, gacc_ref, m_ref, l_ref, acc_ref) = refs[2 * PAGES_PER_STEP:]
    j = pl.program_id(1)
    nstep = pl.num_programs(1)
    ppb = MOBA_BLOCK // PAGE_SIZE

    @pl.when(j == 0)
    def _():
        row = lax.broadcasted_iota(jnp.int32, (LANES, DIFF_W), 0)
        sub_of_lane = lax.broadcasted_iota(jnp.int32, (LANES, DIFF_W), 1) // HEAD_DIM
        qm = jnp.where(row == sub_of_lane, jnp.broadcast_to(q_ref[:, MOBA_W:], (LANES, DIFF_W)), 0.0)
        qb_ref[...] = (qm * ATTN_SCALE).T.astype(BF16)
        m_ref[...] = jnp.full(m_ref.shape, M_INIT, F32)
        l_ref[...] = jnp.zeros(l_ref.shape, F32)
        acc_ref[...] = jnp.zeros(acc_ref.shape, F32)

    far = tail_ref[0:1, :]
    ksum = None
    page_m, page_l, page_acc = [], [], []
    for pg in range(PAGES_PER_STEP):
        k_ref, v_ref = k_refs[pg], v_refs[pg]
        part = jnp.sum(k_ref[:, :MOBA_W], axis=0, keepdims=True)
        ksum = part if pg % ppb == 0 else ksum + part
        if pg % ppb == ppb - 1:
            blk = j * (PAGES_PER_STEP // ppb) + pg // ppb
            gacc_ref[pl.ds(blk, 1), :] = _mxu_round(ksum * (1.0 / MOBA_BLOCK)) * _mxu_round(q_ref[:, :MOBA_W])
        s = _dot(k_ref[:, MOBA_W:], qb_ref[...])
        tail_pg = pg - (PAGES_PER_STEP - ppb)
        if tail_pg >= 0:
            bias = jnp.where(j == nstep - 1, tail_ref[tail_pg * PAGE_SIZE:(tail_pg + 1) * PAGE_SIZE, :], far)
        else:
            bias = far
        s = s + bias
        m_pg = _colmax(s)
        p = jnp.exp(s - m_pg)
        page_m.append(m_pg)
        page_l.append(_fold8(p))
        pr = _mxu_round(p)
        vr = _mxu_round(v_ref[...])
        page_acc.append([_fold8(pr[:, r:r + 1] * vr[:, (r // 2) * DIFF_VDIM:(r // 2 + 1) * DIFF_VDIM])
                         for r in range(N_SUB)])
    m_old = m_ref[...]
    m_new = m_old
    for m_pg in page_m:
        m_new = jnp.maximum(m_new, m_pg)
    alpha = jnp.exp(m_old - m_new)
    weights = [jnp.exp(m_pg - m_new) for m_pg in page_m]
    l_ref[...] = alpha * l_ref[...] + sum(w * l for w, l in zip(weights, page_l))
    m_ref[...] = m_new
    for r in range(N_SUB):
        acc_ref[r] = alpha[:, r:r + 1] * acc_ref[r] + sum(w[:, r:r + 1] * acc[r]
                                                          for w, acc in zip(weights, page_acc))

    @pl.when(j == nstep - 1)
    def _():
        kd = jnp.broadcast_to(knew_ref[:, MOBA_W:], (SUBLANES, DIFF_W))
        s_self = _dot(kd, qb_ref[...])[0:1] + bias0_ref[...]
        m_old = m_ref[...]
        m_new = jnp.maximum(m_old, s_self)
        alpha = jnp.exp(m_old - m_new)
        p_self = jnp.exp(s_self - m_new)
        l = alpha * jnp.sum(l_ref[...], axis=0, keepdims=True) + p_self
        pr_self = _mxu_round(p_self)
        v_self = _mxu_round(vnew_ref[:, MOBA_W:])
        lam = _lambda(lam_ref, lam_init)
        for h in range(DIFF_HEADS):
            vcols = slice(h * DIFF_VDIM, (h + 1) * DIFF_VDIM)
            parts = []
            for c in range(2):
                r = 2 * h + c
                num = (alpha[:, r:r + 1] * jnp.sum(acc_ref[r], axis=0, keepdims=True)
                       + pr_self[:, r:r + 1] * v_self[:, vcols])
                parts.append(num / l[:, r:r + 1])
            oh = parts[0] - lam * parts[1]
            o_ref[:, vcols] = (oh * lax.rsqrt(jnp.mean(oh * oh, -1, keepdims=True) + LN_EPS)
                               * g_ref[...] * (1.0 - lam_init))
        head_row = lax.broadcasted_iota(jnp.int32, (MOBA_HEADS, MOBA_W), 0)
        head_of_lane = lax.broadcasted_iota(jnp.int32, (MOBA_HEADS, MOBA_W), 1) // HEAD_DIM
        seg = jnp.where(head_row == head_of_lane, 1.0, 0.0)
        gate = lax.dot_general(seg, gacc_ref[...], (((1,), (1,)), ((), ())), precision=HIGHEST,
                               preferred_element_type=F32)
        ids = lax.broadcasted_iota(jnp.int32, gate.shape, 1).astype(F32)
        out_lane = lax.broadcasted_iota(jnp.int32, (MOBA_HEADS, LANES), 1)
        picked = jnp.zeros((MOBA_HEADS, LANES), F32)
        for t in range(MOBA_TOPK):
            mx = jnp.max(gate, axis=-1, keepdims=True)
            idx = jnp.min(jnp.where(gate == mx, ids, float(gate.shape[-1])), axis=-1, keepdims=True)
            picked = jnp.where(out_lane == t, idx, picked)
            gate = jnp.where(ids == idx, NEG_INF, gate)
        sel_ref[...] = picked.astype(jnp.int32)


def _sample_main(page_table, q_s, k_s, v_s, cache_k, cache_v, layer, tab_d, lam_vecs, subln_g, lam_init):
    db, n_pages = page_table.shape
    past = n_pages * PAGE_SIZE
    assert n_pages % PAGES_PER_STEP == 0 and past // MOBA_BLOCK >= MOBA_TOPK
    nblk = past // MOBA_BLOCK
    nstep = n_pages // PAGES_PER_STEP
    tab_sub = jnp.repeat(tab_d, 2, axis=0)
    tail = _bias_lookup(tab_sub, MOBA_BLOCK - jnp.arange(MOBA_BLOCK)).T
    tail = jnp.pad(tail, ((0, 0), (0, LANES - N_SUB)))
    bias_self = jnp.pad(tab_sub[:, 0].reshape(1, N_SUB), ((0, 0), (0, LANES - N_SUB)))

    def page(pg, col):
        return lambda b, j, pt: (pt[b * n_pages + PAGES_PER_STEP * j + pg], layer, 0, col)

    tok = lambda b, j, pt: (b, 0, 0)
    const2 = lambda b, j, pt: (0, 0)
    grid_spec = pltpu.PrefetchScalarGridSpec(
        num_scalar_prefetch=1,
        grid=(db, nstep),
        in_specs=([pl.BlockSpec((None, 1, MIX_W), tok)] * 3
                  + [pl.BlockSpec((None, None, PAGE_SIZE, MIX_W), page(pg, 0)) for pg in range(PAGES_PER_STEP)]
                  + [pl.BlockSpec((None, None, PAGE_SIZE, DIFF_W), page(pg, 1)) for pg in range(PAGES_PER_STEP)]
                  + [pl.BlockSpec((MOBA_BLOCK, LANES), const2), pl.BlockSpec((1, LANES), const2),
                     pl.BlockSpec((4, HEAD_DIM), const2), pl.BlockSpec((1, DIFF_VDIM), const2)]),
        out_specs=[pl.BlockSpec((None, 1, DIFF_W), tok),
                   pl.BlockSpec((None, MOBA_HEADS, LANES), tok)],
        scratch_shapes=[pltpu.VMEM((DIFF_W, LANES), BF16), pltpu.VMEM((nblk, MOBA_W), F32),
                        pltpu.VMEM((1, LANES), F32), pltpu.VMEM((SUBLANES, LANES), F32),
                        pltpu.VMEM((N_SUB, SUBLANES, DIFF_VDIM), F32)],
    )
    o_d, sel = pl.pallas_call(
        functools.partial(_sample_main_kernel, lam_init=lam_init),
        grid_spec=grid_spec,
        out_shape=[jax.ShapeDtypeStruct((db, 1, DIFF_W), F32),
                   jax.ShapeDtypeStruct((db, MOBA_HEADS, LANES), jnp.int32)],
        compiler_params=_cparams(("parallel", "arbitrary")),
        name="sample_main",
    )(page_table.reshape(-1), q_s, k_s, v_s, *([cache_k] * PAGES_PER_STEP), *([cache_v] * PAGES_PER_STEP),
      tail, bias_self, lam_vecs, subln_g)
    return o_d.reshape(db, DIFF_W), sel[:, :, :MOBA_TOPK]


def _sample_moba_kernel(pt_ref, sel_ref, q_ref, knew_ref, vnew_ref, *refs):
    del pt_ref, sel_ref
    n_pg = MOBA_TOPK * (MOBA_BLOCK // PAGE_SIZE)
    k_refs = refs[:n_pg]
    v_refs = refs[n_pg:2 * n_pg]
    bias_refs = refs[2 * n_pg:2 * n_pg + MOBA_TOPK]
    bias0_ref, o_ref = refs[2 * n_pg + MOBA_TOPK:]
    h = pl.program_id(1)
    par = h % 2
    lane = lax.broadcasted_iota(jnp.int32, (1, LANES), 1)
    mine = (lane >= par * HEAD_DIM) & (lane < (par + 1) * HEAD_DIM)
    qm = jnp.where(mine, q_ref[...], 0.0) * ATTN_SCALE
    q_rep = jnp.broadcast_to(qm, (LANES, LANES)).T.astype(BF16)
    m = jnp.full((1, LANES), M_INIT, F32)
    l8 = jnp.zeros((SUBLANES, LANES), F32)
    acc = jnp.zeros((SUBLANES, LANES), F32)
    ppb = MOBA_BLOCK // PAGE_SIZE
    for pg in range(n_pg):
        brow = bias_refs[pg // ppb][:, (pg % ppb) * PAGE_SIZE:(pg % ppb + 1) * PAGE_SIZE]
        s = _dot(k_refs[pg][...], q_rep)
        s = s + jnp.broadcast_to(brow, (PAGE_SIZE, PAGE_SIZE)).T
        m_new = jnp.maximum(m, _colmax(s))
        p = jnp.exp(s - m_new)
        alpha = jnp.exp(m - m_new)
        l8 = alpha * l8 + _fold8(p)
        acc = alpha * acc + _fold8(_mxu_round(p) * _mxu_round(v_refs[pg][...]))
        m = m_new
    s_self = jnp.sum(_mxu_round(qm) * _mxu_round(knew_ref[...]), axis=-1, keepdims=True) + bias0_ref[:, 0:1]
    m_new = jnp.maximum(m, s_self)
    alpha = jnp.exp(m - m_new)
    p_self = jnp.exp(s_self - m_new)
    o = ((alpha * jnp.sum(acc, axis=0, keepdims=True) + _mxu_round(p_self) * _mxu_round(vnew_ref[...]))
         / (alpha * jnp.sum(l8, axis=0, keepdims=True) + p_self))
    o_ref[...] = jnp.where(par == 0, o[:, :HEAD_DIM], o[:, HEAD_DIM:])


def _sample_moba(page_table, sel, q_s, k_s, v_s, cache_k, cache_v, layer, tab_m):
    db, n_pages = page_table.shape
    past = n_pages * PAGE_SIZE
    ppb = MOBA_BLOCK // PAGE_SIZE
    n_pg = MOBA_TOPK * ppb
    bias_past = _bias_lookup(tab_m, past - jnp.arange(past)).reshape(MOBA_HEADS, 1, past)
    bias_self = jnp.broadcast_to(tab_m[:, 0:1, None], (MOBA_HEADS, 1, LANES))

    def sel_at(b, h, t, sl):
        return sl[(b * MOBA_HEADS + h) * MOBA_TOPK + t]

    def page(pg):
        return lambda b, h, pt, sl: (pt[b * n_pages + ppb * sel_at(b, h, pg // ppb, sl) + pg % ppb], layer, 0, h // 2)

    def bias_blk(t):
        return lambda b, h, pt, sl: (h, 0, sel_at(b, h, t, sl))

    tok = lambda b, h, pt, sl: (b, 0, h // 2)
    grid_spec = pltpu.PrefetchScalarGridSpec(
        num_scalar_prefetch=2,
        grid=(db, MOBA_HEADS),
        in_specs=([pl.BlockSpec((None, 1, LANES), tok)] * 3
                  + [pl.BlockSpec((None, None, PAGE_SIZE, LANES), page(pg)) for pg in range(n_pg)] * 2
                  + [pl.BlockSpec((None, 1, MOBA_BLOCK), bias_blk(t)) for t in range(MOBA_TOPK)]
                  + [pl.BlockSpec((None, 1, LANES), lambda b, h, pt, sl: (h, 0, 0))]),
        out_specs=pl.BlockSpec((None, None, 1, HEAD_DIM), lambda b, h, pt, sl: (b, h, 0, 0)),
    )
    o_m = pl.pallas_call(
        _sample_moba_kernel,
        grid_spec=grid_spec,
        out_shape=jax.ShapeDtypeStruct((db, MOBA_HEADS, 1, HEAD_DIM), F32),
        compiler_params=_cparams(("parallel", "arbitrary")),
        name="sample_moba",
    )(page_table.reshape(-1), sel.reshape(-1), q_s, k_s, v_s,
      *([cache_k] * n_pg), *([cache_v] * n_pg), *([bias_past] * MOBA_TOPK), bias_self)
    return o_m.reshape(db, MOBA_W)


def _post_attn_kernel(h_ref, mm_ref, md_ref, wo_ref, g_ref, b_ref, wr_ref, br_ref,
                      a_ref, eid_ref, gate_ref, *, alpha, transposed):
    if transposed:
        mm = mm_ref[...].astype(F32).T
        md = md_ref[...].astype(F32).T
    else:
        mm, md = mm_ref[...], md_ref[...]
    mix = _dot(mm, wo_ref[:MOBA_W, :]) + _dot(md, wo_ref[MOBA_W:, :])
    a = _layer_norm(alpha * h_ref[...] + mix, g_ref[...], b_ref[...])
    _store_tiles(a_ref, a)
    logits = _dot(a, wr_ref[...]) + br_ref[...]
    lane = lax.broadcasted_iota(jnp.int32, logits.shape, 1).astype(F32)
    none = float(LANES)
    gl = jnp.where(lane < N_GROUPS, logits, NEG_INF)
    gmax = jnp.max(gl, axis=-1, keepdims=True)
    g_gate = 1.0 / jnp.sum(jnp.exp(gl - gmax), axis=-1, keepdims=True)
    g_idx = jnp.min(jnp.where(gl == gmax, lane, none), axis=-1, keepdims=True)
    lo = N_GROUPS + EXPERTS_PER_GROUP * g_idx
    el = jnp.where(lane >= lo, jnp.where(lane < lo + EXPERTS_PER_GROUP, logits, NEG_INF), NEG_INF)
    m1 = jnp.max(el, axis=-1, keepdims=True)
    i1 = jnp.min(jnp.where(el == m1, lane, none), axis=-1, keepdims=True)
    el2 = jnp.where(lane == i1, NEG_INF, el)
    m2 = jnp.max(el2, axis=-1, keepdims=True)
    i2 = jnp.min(jnp.where(el2 == m2, lane, none), axis=-1, keepdims=True)
    r = jnp.exp(m2 - m1)
    w1 = g_gate / (1.0 + r)
    w2 = g_gate * r / (1.0 + r)
    eid_ref[...] = jnp.where(lane == 0.0, i1 - N_GROUPS, jnp.where(lane == 1.0, i2 - N_GROUPS, 0.0)).astype(jnp.int32)
    gate_ref[...] = jnp.where(lane == 0.0, w1, jnp.where(lane == 1.0, w2, 0.0))


def _post_attn(h, mix_m, mix_d, w_out, ln_g, ln_b, w_router, b_router, alpha, tm, transposed):
    m = h.shape[0]
    row = lambda i: (i, 0)
    const = lambda i: (0, 0)
    if transposed:
        assert tm == T
        mix_specs = [pl.BlockSpec((None, MOBA_W, T), lambda i: (i, 0, 0)),
                     pl.BlockSpec((None, DIFF_W, T), lambda i: (i, 0, 0))]
    else:
        mix_specs = [pl.BlockSpec((tm, MOBA_W), row), pl.BlockSpec((tm, DIFF_W), row)]
    return pl.pallas_call(
        functools.partial(_post_attn_kernel, alpha=alpha, transposed=transposed),
        grid=(m // tm,),
        in_specs=[pl.BlockSpec((tm, D_MODEL), row)] + mix_specs + [
            pl.BlockSpec((MIX_W, D_MODEL), const),
            pl.BlockSpec((1, D_MODEL), const), pl.BlockSpec((1, D_MODEL), const),
            pl.BlockSpec((D_MODEL, LANES), const), pl.BlockSpec((1, LANES), const)],
        out_specs=[pl.BlockSpec((tm * N_CHUNK, LANES), row), pl.BlockSpec((tm, LANES), row),
                   pl.BlockSpec((tm, LANES), row)],
        out_shape=[jax.ShapeDtypeStruct((m * N_CHUNK, LANES), F32), jax.ShapeDtypeStruct((m, LANES), jnp.int32),
                   jax.ShapeDtypeStruct((m, LANES), F32)],
        compiler_params=_cparams(("parallel",)),
        name="post_attn",
    )(h, mix_m, mix_d, w_out, ln_g, ln_b, w_router, b_router)


def _moe_kernel(code_ref, blk_e_ref, blk_src_ref, blk_n_ref, nblk_ref, a_hbm, w1_ref, w3_ref, w2_ref, y_hbm,
                xbuf, ybuf, w1b, w3b, w2b, gsem, ssem, *, rb, n_tok):
    i = pl.program_id(0)
    n_used = nblk_ref[0]

    def tile(t):
        start = t * N_CHUNK
        return pl.ds(start if isinstance(t, int) else pl.multiple_of(start, N_CHUNK), N_CHUNK)

    def gather_row(base, r, slot):
        tok = code_ref[base + r] >> 1
        return pltpu.make_async_copy(a_hbm.at[tile(tok)], xbuf.at[slot, tile(r)], gsem.at[slot])

    def scatter_row(base, r, n_valid, slot):
        code = code_ref[base + r]
        dst = jnp.where(r < n_valid, (code >> 1) + (code & 1) * n_tok, 2 * n_tok + slot * rb + r)
        return pltpu.make_async_copy(ybuf.at[slot, tile(r)], y_hbm.at[tile(dst)], ssem.at[slot])

    def for_rows(fn):
        def body(r, carry):
            fn(r)
            return carry
        lax.fori_loop(0, rb, body, 0, unroll=SUBLANES)

    def wait_gather(slot):
        pltpu.make_async_copy(xbuf.at[slot], xbuf.at[slot], gsem.at[slot]).wait()

    def wait_scatter(slot):
        pltpu.make_async_copy(ybuf.at[slot], ybuf.at[slot], ssem.at[slot]).wait()

    def step(slot):
        other = 1 - slot

        def first_step():
            base0 = blk_src_ref[0]
            for_rows(lambda r: gather_row(base0, r, 0).start())
            ybuf[1] = jnp.zeros(ybuf.shape[1:], F32)
            for half in range(2):
                for_rows(lambda r, half=half: pltpu.make_async_copy(
                    ybuf.at[1, tile(r)], y_hbm.at[tile(2 * n_tok + half * rb + r)], ssem.at[1]).start())
                wait_scatter(1)

        if slot == 0:
            pl.when(i == 0)(first_step)

        e = blk_e_ref[i]

        @pl.when((i == 0) | (e != blk_e_ref[jnp.maximum(i - 1, 0)]))
        def _():
            w1b[...] = w1_ref[...].astype(BF16)
            w3b[...] = w3_ref[...].astype(BF16)
            w2b[...] = w2_ref[...].astype(BF16)

        wait_gather(slot)

        @pl.when(i >= 1)
        def _():
            wait_scatter(slot)

        nxt_base = blk_src_ref[jnp.minimum(i + 1, pl.num_programs(0) - 1)]
        prv = jnp.maximum(i - 1, 0)
        prv_base = blk_src_ref[prv]
        prv_valid = jnp.where(i >= 1, blk_n_ref[prv], 0)

        for_rows(lambda r: gather_row(nxt_base, r, other).start())
        x = _load_tiles(xbuf.at[slot]).astype(BF16)
        h1 = jnp.dot(x, w1b[...], preferred_element_type=F32)
        h3 = jnp.dot(x, w3b[...], preferred_element_type=F32)
        hidden = (h1 * jax.nn.sigmoid(h1) * h3).astype(BF16)
        for r in range(rb):
            scatter_row(prv_base, r, prv_valid, other).start(priority=1)
        y = jnp.dot(hidden, w2b[...], preferred_element_type=F32)
        _store_tiles(ybuf.at[slot], y)

        @pl.when(i == n_used - 1)
        def _():
            base = blk_src_ref[i]
            n_valid = blk_n_ref[i]
            for_rows(lambda r: scatter_row(base, r, n_valid, slot).start())
            wait_scatter(slot)
            wait_scatter(other)
            wait_gather(other)

    for parity in range(2):
        pl.when((i < n_used) & (i % 2 == parity))(functools.partial(step, parity))


def _route_tables(eid, rb):
    n = eid.shape[0]
    m = n * 2
    e_flat = eid.reshape(m)
    keys = lax.sort(e_flat * m + jnp.arange(m, dtype=jnp.int32))
    code = keys % m
    experts = jnp.arange(N_EXPERTS, dtype=jnp.int32)
    counts = jnp.sum((e_flat[:, None] == experts[None, :]).astype(jnp.int32), axis=0)
    padded = (counts + rb - 1) // rb * rb
    start = jnp.cumsum(counts) - counts
    p_end = jnp.cumsum(padded)
    p_start = p_end - padded
    n_blk = -(-(m + N_EXPERTS * (rb - 1)) // rb)
    blk_row = jnp.arange(n_blk, dtype=jnp.int32) * rb
    blk_e = jnp.minimum(jnp.sum((p_end[None, :] <= blk_row[:, None]).astype(jnp.int32), axis=1), N_EXPERTS - 1)
    onehot = (blk_e[:, None] == experts[None, :]).astype(jnp.int32)
    pick = lambda table: jnp.sum(onehot * table[None, :], axis=1)
    off = blk_row - pick(p_start)
    blk_src = jnp.clip(pick(start) + off, 0, m - 1).astype(jnp.int32)
    blk_n = jnp.clip(pick(counts) - off, 0, rb).astype(jnp.int32)
    n_used = (p_end[-1:] // rb).astype(jnp.int32)
    code = jnp.pad(code.astype(jnp.int32), (0, rb))
    return code, blk_e.astype(jnp.int32), blk_src, blk_n, n_used, n_blk


def _moe(a_tiles, eid, w1, w3, w2, rb):
    n = a_tiles.shape[0] // N_CHUNK
    code, blk_e, blk_src, blk_n, n_used, n_blk = _route_tables(eid, rb)
    wmap = lambda i, code, be, bs, bn, nu: (be[i], 0, 0)
    grid_spec = pltpu.PrefetchScalarGridSpec(
        num_scalar_prefetch=5,
        grid=(n_blk,),
        in_specs=[pl.BlockSpec(memory_space=pl.ANY),
                  pl.BlockSpec((None, D_MODEL, D_EXPERT), wmap),
                  pl.BlockSpec((None, D_MODEL, D_EXPERT), wmap),
                  pl.BlockSpec((None, D_EXPERT, D_MODEL), wmap)],
        out_specs=pl.BlockSpec(memory_space=pl.ANY),
        scratch_shapes=[pltpu.VMEM((2, rb * N_CHUNK, LANES), F32), pltpu.VMEM((2, rb * N_CHUNK, LANES), F32),
                        pltpu.VMEM((D_MODEL, D_EXPERT), BF16), pltpu.VMEM((D_MODEL, D_EXPERT), BF16),
                        pltpu.VMEM((D_EXPERT, D_MODEL), BF16),
                        pltpu.SemaphoreType.DMA((2,)), pltpu.SemaphoreType.DMA((2,))],
    )
    return pl.pallas_call(
        functools.partial(_moe_kernel, rb=rb, n_tok=n),
        grid_spec=grid_spec,
        out_shape=jax.ShapeDtypeStruct(((2 * n + 2 * rb) * N_CHUNK, LANES), F32),
        compiler_params=_cparams(("arbitrary",)),
        name="moe",
    )(code, blk_e, blk_src, blk_n, n_used, a_tiles, w1, w3, w2)


def _finish_kernel(a_ref, y0_ref, y1_ref, gate_ref, p_ref, g_ref, b_ref, wpg_ref, bpg_ref, wple_ref, o_ref,
                   *, alpha):
    gates = gate_ref[...]
    moe = _load_tiles(y0_ref) * gates[:, 0:1] + _load_tiles(y1_ref) * gates[:, 1:2]
    c = _layer_norm(alpha * _load_tiles(a_ref) + moe, g_ref[...], b_ref[...])
    gate = jax.nn.sigmoid(_dot(c, wpg_ref[...]) + bpg_ref[...])
    o_ref[...] = c + gate * _dot(p_ref[...], wple_ref[...])


def _finish(a, y, gates, p, ln_g, ln_b, w_pg, b_pg, w_ple, alpha, tm):
    m = a.shape[0] // N_CHUNK
    assert m % tm == 0
    row = lambda i: (i, 0)
    const = lambda i: (0, 0)
    tiles = (tm * N_CHUNK, LANES)
    return pl.pallas_call(
        functools.partial(_finish_kernel, alpha=alpha),
        grid=(m // tm,),
        in_specs=[pl.BlockSpec(tiles, row), pl.BlockSpec(tiles, row),
                  pl.BlockSpec(tiles, lambda i: (m // tm + i, 0)),
                  pl.BlockSpec((tm, LANES), row), pl.BlockSpec((tm, D_PLE), row),
                  pl.BlockSpec((1, D_MODEL), const), pl.BlockSpec((1, D_MODEL), const),
                  pl.BlockSpec((D_MODEL, D_MODEL), const), pl.BlockSpec((1, D_MODEL), const),
                  pl.BlockSpec((D_PLE, D_MODEL), const)],
        out_specs=pl.BlockSpec((tm, D_MODEL), row),
        out_shape=jax.ShapeDtypeStruct((m, D_MODEL), F32),
        compiler_params=_cparams(("parallel",)),
        name="finish",
    )(a, y, y, gates, p, ln_g, ln_b, w_pg, b_pg, w_ple)


def _pack_router(w_group, b_group, w_er, b_er):
    w = jnp.concatenate([w_group, jnp.transpose(w_er, (1, 0, 2)).reshape(D_MODEL, N_EXPERTS)], axis=1)
    b = jnp.concatenate([b_group, b_er.reshape(N_EXPERTS)])
    pad = LANES - w.shape[1]
    return jnp.pad(w, ((0, 0), (0, pad))), jnp.pad(b, (0, pad)).reshape(1, LANES)


def kernel(x_prompt, x_sample, cache_k, cache_v, page_table, p_prompt, p_sample, rel_bias, w_in, w_out, lam_q1, lam_k1, lam_q2, lam_k2, subln_g, ln1_g, ln1_b, w_group, b_group, w_erouter, b_erouter, w1, w3, w2, ln2_g, ln2_b, w_ple, w_pg, b_pg):
    depth = w_in.shape[0]
    b, s, _ = x_prompt.shape
    db, ds, _ = x_sample.shape
    assert ds == 1 and s % T == 0
    alpha = (2 * depth) ** 0.25
    tab_m = rel_bias[:, :MOBA_HEADS].T
    tab_d = rel_bias[:, MOBA_HEADS:].T
    hp = x_prompt.reshape(b * s, D_MODEL)
    hs = x_sample.reshape(db, D_MODEL)
    kp_rows, vp_rows, ks_rows, vs_rows = [], [], [], []
    for i in range(depth):
        lam_init = 0.8 - 0.6 * math.exp(-0.3 * i)
        lam_vecs = jnp.stack([lam_q1[i], lam_k1[i], lam_q2[i], lam_k2[i]]).astype(F32)
        g_sub = subln_g[i].reshape(1, DIFF_VDIM)
        w_router, b_router = _pack_router(w_group[i], b_group[i], w_erouter[i], b_erouter[i])
        ln1 = (ln1_g[i].reshape(1, D_MODEL), ln1_b[i].reshape(1, D_MODEL))
        ln2 = (ln2_g[i].reshape(1, D_MODEL), ln2_b[i].reshape(1, D_MODEL))
        b_pg_i = b_pg[i].reshape(1, D_MODEL)

        w_in_b, w_out_b = w_in[i].astype(BF16), w_out[i].astype(BF16)
        w_pg_b, w_ple_b, w_router_b = w_pg[i].astype(BF16), w_ple[i].astype(BF16), w_router.astype(BF16)

        k_p, v_p, qt, kb, vt, kmean = _qkv_prompt(hp, w_in_b)
        mix_m = _moba_prompt(qt, kb, vt, kmean, tab_m, b, s)
        mix_d = _diff_prompt(qt, kb, vt, tab_d, lam_vecs, g_sub, lam_init, b, s)
        a_p, eid_p, gate_p = _post_attn(hp, mix_m, mix_d, w_out_b, *ln1, w_router_b, b_router, alpha, T, True)
        y_p = _moe(a_p, eid_p[:, :2], w1[i], w3[i], w2[i], 256)
        hp = _finish(a_p, y_p, gate_p, p_prompt[i].reshape(b * s, D_PLE), *ln2, w_pg_b, b_pg_i, w_ple_b,
                     alpha, 512)

        qkv_s = _mm_small(hs, w_in_b, MIX_W)
        q_s = qkv_s[:, :MIX_W].reshape(db, 1, MIX_W)
        k_s = qkv_s[:, MIX_W:2 * MIX_W].reshape(db, 1, MIX_W)
        v_s = qkv_s[:, 2 * MIX_W:].reshape(db, 1, MIX_W)
        o_d, sel = _sample_main(page_table, q_s, k_s, v_s, cache_k, cache_v, i, tab_d, lam_vecs, g_sub, lam_init)
        o_m = _sample_moba(page_table, sel, q_s, k_s, v_s, cache_k, cache_v, i, tab_m)
        a_s, eid_s, gate_s = _post_attn(hs, o_m, o_d, w_out_b, *ln1, w_router_b, b_router, alpha, db, False)
        y_s = _moe(a_s, eid_s[:, :2], w1[i], w3[i], w2[i], 64)
        hs = _finish(a_s, y_s, gate_s, p_sample[i].reshape(db, D_PLE), *ln2, w_pg_b, b_pg_i, w_ple_b, alpha, db)

        kp_rows.append(k_p.reshape(b, s, MIX_W))
        vp_rows.append(v_p.reshape(b, s, MIX_W))
        ks_rows.append(k_s.reshape(db, ds, MIX_W))
        vs_rows.append(v_s.reshape(db, ds, MIX_W))
    return (hp.reshape(b, s, D_MODEL), hs.reshape(db, ds, D_MODEL),
            jnp.stack(kp_rows, axis=1), jnp.stack(vp_rows, axis=1),
            jnp.stack(ks_rows, axis=1), jnp.stack(vs_rows, axis=1))
```

```python
import functools
import math

import jax
import jax.numpy as jnp
from jax import lax
from jax.experimental import pallas as pl
from jax.experimental.pallas import tpu as pltpu

D_MODEL = 1024
PAGE_SIZE = 128
HEAD_DIM = 64
MOBA_HEADS = 8
MOBA_W = MOBA_HEADS * HEAD_DIM
DIFF_HEADS = 4
DIFF_VDIM = 2 * HEAD_DIM
DIFF_W = DIFF_HEADS * DIFF_VDIM
N_SUB = 2 * DIFF_HEADS
MIX_W = MOBA_W + DIFF_W
MOBA_BLOCK = 256
MOBA_TOPK = 3
N_BUCKETS = 32
REL_MAX_DIST = 128
N_GROUPS = 4
EXPERTS_PER_GROUP = 8
N_EXPERTS = N_GROUPS * EXPERTS_PER_GROUP
D_EXPERT = D_MODEL // 2
D_PLE = 256
LN_EPS = 1e-5
ATTN_SCALE = HEAD_DIM ** -0.5
LANES = 128
SUBLANES = 8
N_CHUNK = D_MODEL // LANES
PAGES_PER_STEP = 16
F32 = jnp.float32
BF16 = jnp.bfloat16
HIGHEST = lax.Precision.HIGHEST
NEG_INF = float("-inf")
M_INIT = -1e30
VMEM_LIMIT = 48 * 1024 * 1024
T = MOBA_BLOCK


def _cparams(sem):
    return pltpu.CompilerParams(dimension_semantics=sem, vmem_limit_bytes=VMEM_LIMIT)


def _dot(a, b):
    return jnp.dot(a.astype(BF16), b.astype(BF16), preferred_element_type=F32)


def _mxu_round(x):
    return x.astype(BF16).astype(F32)


def _rel_bucket(rel):
    n = jnp.maximum(rel, 0)
    max_exact = N_BUCKETS // 2
    large = max_exact + (jnp.log(jnp.maximum(n, 1).astype(F32) / max_exact)
                         / math.log(REL_MAX_DIST / max_exact) * (N_BUCKETS - max_exact)).astype(jnp.int32)
    large = jnp.minimum(large, N_BUCKETS - 1)
    return jnp.where(n < max_exact, n, large)


def _bias_lookup(tab, rel):
    onehot = (_rel_bucket(rel)[..., None] == jnp.arange(N_BUCKETS)).astype(F32)
    return jnp.einsum('hb,...b->h...', tab, onehot, precision=HIGHEST)


def _layer_norm(x, g, b):
    xc = x - jnp.mean(x, -1, keepdims=True)
    var = jnp.mean(xc * xc, -1, keepdims=True)
    return xc * lax.rsqrt(var + LN_EPS) * g + b


def _fold8(x):
    return jnp.sum(x.reshape(x.shape[0] // SUBLANES, SUBLANES, x.shape[1]), axis=0)


def _store_tiles(ref, x):
    rows = x.shape[0]
    for j in range(N_CHUNK):
        ref[pl.ds(j, rows, stride=N_CHUNK), :] = x[:, j * LANES:(j + 1) * LANES]


def _load_tiles(ref):
    rows = ref.shape[0] // N_CHUNK
    return jnp.concatenate([ref[pl.ds(j, rows, stride=N_CHUNK), :] for j in range(N_CHUNK)], axis=1)


def _colmax(x):
    part = jnp.max(x.reshape(x.shape[0] // SUBLANES, SUBLANES, x.shape[1]), axis=0)
    return jnp.max(part, axis=0, keepdims=True)


def _qkv_prompt_kernel(x_ref, w_ref, k_ref, v_ref, qt_ref, kb_ref, vt_ref, kmean_ref):
    x = x_ref[...].astype(BF16)
    q = jnp.dot(x, w_ref[:, :MIX_W], preferred_element_type=F32)
    qt_ref[...] = q.T.astype(BF16)
    k = jnp.dot(x, w_ref[:, MIX_W:2 * MIX_W], preferred_element_type=F32)
    k_ref[...] = k
    kb_ref[...] = k.astype(BF16)
    kmean_ref[...] = jnp.sum(k[:, :MOBA_W], axis=0, keepdims=True) * (1.0 / MOBA_BLOCK)
    v = jnp.dot(x, w_ref[:, 2 * MIX_W:], preferred_element_type=F32)
    v_ref[...] = v
    vt_ref[...] = v.T.astype(BF16)


def _qkv_prompt(x, w_bf16):
    m = x.shape[0]
    nt = m // T
    row = lambda i: (i, 0)
    tile3 = lambda i: (i, 0, 0)
    return pl.pallas_call(
        _qkv_prompt_kernel,
        grid=(nt,),
        in_specs=[pl.BlockSpec((T, D_MODEL), row),
                  pl.BlockSpec((D_MODEL, 3 * MIX_W), lambda i: (0, 0))],
        out_specs=[pl.BlockSpec((T, MIX_W), row), pl.BlockSpec((T, MIX_W), row),
                   pl.BlockSpec((None, MIX_W, T), tile3), pl.BlockSpec((T, MIX_W), row),
                   pl.BlockSpec((None, MIX_W, T), tile3),
                   pl.BlockSpec((None, 1, MOBA_W), tile3)],
        out_shape=[jax.ShapeDtypeStruct((m, MIX_W), F32), jax.ShapeDtypeStruct((m, MIX_W), F32),
                   jax.ShapeDtypeStruct((nt, MIX_W, T), BF16), jax.ShapeDtypeStruct((m, MIX_W), BF16),
                   jax.ShapeDtypeStruct((nt, MIX_W, T), BF16),
                   jax.ShapeDtypeStruct((nt, 1, MOBA_W), F32)],
        compiler_params=_cparams(("parallel",)),
        name="qkv_prompt",
    )(x, w_bf16)


def _mm_small_kernel(x_ref, w_ref, o_ref):
    o_ref[...] = _dot(x_ref[...], w_ref[...])


def _mm_small(x, w, tn):
    m, k = x.shape
    n = w.shape[1]
    return pl.pallas_call(
        _mm_small_kernel,
        grid=(n // tn,),
        in_specs=[pl.BlockSpec((m, k), lambda j: (0, 0)), pl.BlockSpec((k, tn), lambda j: (0, j))],
        out_specs=pl.BlockSpec((m, tn), lambda j: (0, j)),
        out_shape=jax.ShapeDtypeStruct((m, n), F32),
        compiler_params=_cparams(("parallel",)),
        name="mm_small",
    )(x, w)


def _stream_blocks(i, n_streams, scores, values, keep_row, bias_d, bias_p,
                   m_ref, l_ref, acc_ref, s_ref, p_ref, a_ref, dv):
    m_ref[...] = jnp.full(m_ref.shape, M_INIT, F32)
    l_ref[...] = jnp.zeros(l_ref.shape, F32)
    acc_ref[...] = jnp.zeros(acc_ref.shape, F32)

    def block(j, finish_scores, shift_of, gated, stage):
        col_max = []
        for r in range(n_streams):
            s = finish_scores(r, j, scores(r, j))
            s_ref[stage, r] = s
            col_max.append(_colmax(s))
        alphas = []
        for r in range(n_streams):
            shift = shift_of(r)
            m_old = m_ref[r]
            tile_max = col_max[r] + shift
            keep = keep_row(r, j) if gated else None
            if keep is not None:
                tile_max = jnp.where(keep > 0.5, tile_max, NEG_INF)
            m_new = jnp.maximum(m_old, tile_max)
            offset = m_new - shift
            if keep is not None:
                offset = jnp.where(keep > 0.5, offset, float("inf"))
            p = jnp.exp(s_ref[stage, r] - offset)
            alpha = jnp.exp(m_old - m_new)
            p_ref[stage, r] = p.astype(p_ref.dtype)
            m_ref[r] = m_new
            alphas.append(alpha)
        ones = jnp.ones((2 * SUBLANES, T), p_ref.dtype)
        for r in range(n_streams):
            rows = slice(r * dv, (r + 1) * dv)
            out = jnp.dot(jnp.concatenate([values(r, j), ones], axis=0), p_ref[stage, r],
                          preferred_element_type=F32)
            acc_ref[rows, :] = alphas[r] * acc_ref[rows, :] + out[:dv]
            l_ref[r] = alphas[r] * l_ref[r] + out[dv:dv + 1]

    n_far = jnp.maximum(i - 1, 0)
    last_far = jnp.maximum(n_far - 1, 0)

    def qk(j, slot):
        for r in range(n_streams):
            s_ref[slot, r] = scores(r, j)

    def softmax(t, slot):
        live = jnp.where(t < n_far, 0.0, NEG_INF).astype(F32).reshape(1, 1)
        for r in range(n_streams):
            shift = bias_p(r)[0:1, 0:1] + live
            s = s_ref[slot, r]
            m_old = m_ref[r]
            tile_max = _colmax(s) + shift
            keep = keep_row(r, t)
            if keep is not None:
                tile_max = jnp.where(keep > 0.5, tile_max, NEG_INF)
            m_new = jnp.maximum(m_old, tile_max)
            offset = m_new - shift
            if keep is not None:
                offset = jnp.where(keep > 0.5, offset, float("inf"))
            p_ref[slot, r] = jnp.exp(s - offset).astype(p_ref.dtype)
            a_ref[slot, r] = jnp.exp(m_old - m_new)
            m_ref[r] = m_new

    def pv(j, slot):
        ones = jnp.ones((2 * SUBLANES, T), p_ref.dtype)
        for r in range(n_streams):
            rows = slice(r * dv, (r + 1) * dv)
            out = jnp.dot(jnp.concatenate([values(r, j), ones], axis=0), p_ref[slot, r],
                          preferred_element_type=F32)
            alpha = a_ref[slot, r]
            acc_ref[rows, :] = alpha * acc_ref[rows, :] + out[:dv]
            l_ref[r] = alpha * l_ref[r] + out[dv:dv + 1]

    p_ref[1] = jnp.zeros(p_ref.shape[1:], p_ref.dtype)
    a_ref[1] = jnp.ones(a_ref.shape[1:], F32)
    qk(0, 0)

    def far_pair(u, carry):
        for slot in range(2):
            t = 2 * u + slot
            qk(jnp.minimum(t + 1, last_far), 1 - slot)
            softmax(t, slot)
            pv(jnp.maximum(t - 1, 0), 1 - slot)
        return carry

    n_pairs = (n_far + 1) // 2
    lax.fori_loop(0, n_pairs, far_pair, 0)
    pv(jnp.maximum(2 * n_pairs - 1, 0), 1)

    prev_shift = jnp.where(i >= 1, 0.0, NEG_INF).astype(F32).reshape(1, 1)
    block(jnp.maximum(i - 1, 0), lambda r, j, s: s + bias_p(r)[...], lambda r: prev_shift, True, 0)

    key = lax.broadcasted_iota(jnp.int32, (T, T), 0)
    qry = lax.broadcasted_iota(jnp.int32, (T, T), 1)
    zero = jnp.zeros((1, 1), F32)
    block(i, lambda r, j, s: jnp.where(qry >= key, s + bias_d(r)[...], NEG_INF), lambda r: zero, False, 1)


def _masked_half(qt_slab, half):
    row = lax.broadcasted_iota(jnp.int32, qt_slab.shape, 0)
    mine = (row >= half * HEAD_DIM) & (row < (half + 1) * HEAD_DIM)
    return jnp.where(mine, qt_slab, jnp.zeros_like(qt_slab))


def _top_blocks_t(gate_t, n_valid):
    nb = gate_t.shape[0]
    ids = lax.broadcasted_iota(jnp.int32, gate_t.shape, 0).astype(F32)
    g = jnp.where(ids < n_valid.astype(F32), gate_t, NEG_INF)
    sel = jnp.zeros(gate_t.shape, F32)
    for _ in range(MOBA_TOPK):
        mx = jnp.max(g, axis=0, keepdims=True)
        cand = jnp.where(g == mx, jnp.where(mx > NEG_INF, ids, float(nb)), float(nb))
        pick = ids == jnp.min(cand, axis=0, keepdims=True)
        sel = jnp.where(pick, 1.0, sel)
        g = jnp.where(pick, NEG_INF, g)
    return sel


def _moba_prompt_kernel(qt_ref, k_ref, vt_ref, kmean_ref, bd_ref, bp_ref, o_ref,
                        qs_ref, sel_ref, m_ref, l_ref, acc_ref, s_ref, p_ref, a_ref):
    i = pl.program_id(1)
    for h in range(MOBA_HEADS):
        slab = slice((h // 2) * LANES, (h // 2 + 1) * LANES)
        qh = _masked_half(qt_ref[slab, :], h % 2)
        gate_t = jnp.dot(kmean_ref[:, slab].astype(BF16), qh, preferred_element_type=F32)
        sel_ref[h] = _top_blocks_t(gate_t, i)
        qs_ref[h] = qh * ATTN_SCALE

    def scores(h, j):
        rows = pl.ds(pl.multiple_of(j * T, T), T)
        slab = slice((h // 2) * LANES, (h // 2 + 1) * LANES)
        return jnp.dot(k_ref[rows, slab], qs_ref[h], preferred_element_type=F32)

    _stream_blocks(i, MOBA_HEADS, scores,
                   lambda h, j: vt_ref[j, h * HEAD_DIM:(h + 1) * HEAD_DIM, :],
                   lambda h, j: sel_ref[h, pl.ds(j, 1), :],
                   lambda h: bd_ref.at[h], lambda h: bp_ref.at[h], m_ref, l_ref, acc_ref, s_ref, p_ref, a_ref,
                   HEAD_DIM)
    for h in range(MOBA_HEADS):
        rows = slice(h * HEAD_DIM, (h + 1) * HEAD_DIM)
        o_ref[rows, :] = (acc_ref[rows, :] / l_ref[h]).astype(o_ref.dtype)


def _lambda(lam_ref, lam_init):
    lv = lam_ref[...]
    s1 = jnp.sum(lv[0:1] * lv[1:2], axis=-1, keepdims=True)
    s2 = jnp.sum(lv[2:3] * lv[3:4], axis=-1, keepdims=True)
    return jnp.exp(s1) - jnp.exp(s2) + lam_init


def _diff_prompt_kernel(qt_ref, k_ref, vt_ref, bd_ref, bp_ref, lam_ref, g_ref, o_ref,
                        qs_ref, m_ref, l_ref, acc_ref, s_ref, p_ref, a_ref, *, lam_init):
    i = pl.program_id(1)
    for r in range(N_SUB):
        slab = slice((r // 2) * LANES, (r // 2 + 1) * LANES)
        qs_ref[r] = _masked_half(qt_ref[slab, :], r % 2) * ATTN_SCALE

    def scores(r, j):
        rows = pl.ds(pl.multiple_of(j * T, T), T)
        slab = slice((r // 2) * LANES, (r // 2 + 1) * LANES)
        return jnp.dot(k_ref[rows, slab], qs_ref[r], preferred_element_type=F32)

    _stream_blocks(i, N_SUB, scores,
                   lambda r, j: vt_ref[j, (r // 2) * DIFF_VDIM:(r // 2 + 1) * DIFF_VDIM, :],
                   lambda r, j: None,
                   lambda r: bd_ref.at[r // 2], lambda r: bp_ref.at[r // 2], m_ref, l_ref, acc_ref, s_ref, p_ref, a_ref,
                   DIFF_VDIM)
    lam = _lambda(lam_ref, lam_init)
    for h in range(DIFF_HEADS):
        parts = []
        for c in range(2):
            r = 2 * h + c
            parts.append(acc_ref[r * DIFF_VDIM:(r + 1) * DIFF_VDIM, :] / l_ref[r])
        o = parts[0] - lam * parts[1]
        o = o * lax.rsqrt(jnp.mean(o * o, axis=0, keepdims=True) + LN_EPS) * g_ref[...] * (1.0 - lam_init)
        o_ref[h * DIFF_VDIM:(h + 1) * DIFF_VDIM, :] = o.astype(o_ref.dtype)


def _prompt_bias_tiles(tab):
    key = jnp.arange(T)[:, None]
    qry = jnp.arange(T)[None, :]
    return _bias_lookup(tab, qry - key), _bias_lookup(tab, qry - key + T)


def _moba_prompt(qt, kb, vt, kmean, tab_m, b, s):
    nq = s // T
    bd, bp = _prompt_bias_tiles(tab_m)
    const3 = lambda bi, i: (0, 0, 0)
    return pl.pallas_call(
        _moba_prompt_kernel,
        grid=(b, nq),
        in_specs=[pl.BlockSpec((None, MOBA_W, T), lambda bi, i: (bi * nq + i, 0, 0)),
                  pl.BlockSpec((s, MOBA_W), lambda bi, i: (bi, 0)),
                  pl.BlockSpec((nq, MOBA_W, T), lambda bi, i: (bi, 0, 0)),
                  pl.BlockSpec((None, nq, MOBA_W), lambda bi, i: (bi, 0, 0)),
                  pl.BlockSpec((MOBA_HEADS, T, T), const3),
                  pl.BlockSpec((MOBA_HEADS, T, T), const3)],
        out_specs=pl.BlockSpec((None, MOBA_W, T), lambda bi, i: (bi * nq + i, 0, 0)),
        out_shape=jax.ShapeDtypeStruct((b * nq, MOBA_W, T), BF16),
        scratch_shapes=[pltpu.VMEM((MOBA_HEADS, LANES, T), BF16), pltpu.VMEM((MOBA_HEADS, nq, T), F32),
                        pltpu.VMEM((MOBA_HEADS, 1, T), F32), pltpu.VMEM((MOBA_HEADS, 1, T), F32),
                        pltpu.VMEM((MOBA_W, T), F32),
                        pltpu.VMEM((2, MOBA_HEADS, T, T), F32), pltpu.VMEM((2, MOBA_HEADS, T, T), BF16),
                        pltpu.VMEM((2, MOBA_HEADS, 1, T), F32)],
        compiler_params=_cparams(("parallel", "arbitrary")),
        name="moba_prompt",
    )(qt, kb, vt, kmean.reshape(b, nq, MOBA_W), bd, bp)


def _diff_prompt(qt, kb, vt, tab_d, lam_vecs, subln_g, lam_init, b, s):
    nq = s // T
    bd, bp = _prompt_bias_tiles(tab_d)
    g_cols = jnp.broadcast_to(subln_g.reshape(DIFF_VDIM, 1), (DIFF_VDIM, T))
    const3 = lambda bi, i: (0, 0, 0)
    const2 = lambda bi, i: (0, 0)
    return pl.pallas_call(
        functools.partial(_diff_prompt_kernel, lam_init=lam_init),
        grid=(b, nq),
        in_specs=[pl.BlockSpec((None, DIFF_W, T), lambda bi, i: (bi * nq + i, 1, 0)),
                  pl.BlockSpec((s, DIFF_W), lambda bi, i: (bi, 1)),
                  pl.BlockSpec((nq, DIFF_W, T), lambda bi, i: (bi, 1, 0)),
                  pl.BlockSpec((DIFF_HEADS, T, T), const3),
                  pl.BlockSpec((DIFF_HEADS, T, T), const3),
                  pl.BlockSpec((4, HEAD_DIM), const2),
                  pl.BlockSpec((DIFF_VDIM, T), const2)],
        out_specs=pl.BlockSpec((None, DIFF_W, T), lambda bi, i: (bi * nq + i, 0, 0)),
        out_shape=jax.ShapeDtypeStruct((b * nq, DIFF_W, T), BF16),
        scratch_shapes=[pltpu.VMEM((N_SUB, LANES, T), BF16),
                        pltpu.VMEM((N_SUB, 1, T), F32), pltpu.VMEM((N_SUB, 1, T), F32),
                        pltpu.VMEM((N_SUB * DIFF_VDIM, T), F32),
                        pltpu.VMEM((2, N_SUB, T, T), F32), pltpu.VMEM((2, N_SUB, T, T), BF16),
                        pltpu.VMEM((2, N_SUB, 1, T), F32)],
        compiler_params=_cparams(("parallel", "arbitrary")),
        name="diff_prompt",
    )(qt, kb, vt, bd, bp, lam_vecs, g_cols)


def _sample_main_kernel(pt_ref, q_ref, knew_ref, vnew_ref, *refs, lam_init):
    del pt_ref
    k_refs = refs[:PAGES_PER_STEP]
    v_refs = refs[PAGES_PER_STEP:2 * PAGES_PER_STEP]
    (tail_ref, bias0_ref, lam_ref, g_ref, o_ref, sel_ref,
     qb_ref, gacc_ref, m_ref, l_ref, acc_ref) = refs[2 * PAGES_PER_STEP:]
    j = pl.program_id(1)
    nstep = pl.num_programs(1)
    ppb = MOBA_BLOCK // PAGE_SIZE

    @pl.when(j == 0)
    def _():
        row = lax.broadcasted_iota(jnp.int32, (LANES, DIFF_W), 0)
        sub_of_lane = lax.broadcasted_iota(jnp.int32, (LANES, DIFF_W), 1) // HEAD_DIM
        qm = jnp.where(row == sub_of_lane, jnp.broadcast_to(q_ref[:, MOBA_W:], (LANES, DIFF_W)), 0.0)
        qb_ref[...] = (qm * ATTN_SCALE).T.astype(BF16)
        m_ref[...] = jnp.full(m_ref.shape, M_INIT, F32)
        l_ref[...] = jnp.zeros(l_ref.shape, F32)
        acc_ref[...] = jnp.zeros(acc_ref.shape, F32)

    far = tail_ref[0:1, :]
    ksum = None
    page_m, page_l, page_acc = [], [], []
    for pg in range(PAGES_PER_STEP):
        k_ref, v_ref = k_refs[pg], v_refs[pg]
        part = jnp.sum(k_ref[:, :MOBA_W], axis=0, keepdims=True)
        ksum = part if pg % ppb == 0 else ksum + part
        if pg % ppb == ppb - 1:
            blk = j * (PAGES_PER_STEP // ppb) + pg // ppb
            gacc_ref[pl.ds(blk, 1), :] = _mxu_round(ksum * (1.0 / MOBA_BLOCK)) * _mxu_round(q_ref[:, :MOBA_W])
        s = _dot(k_ref[:, MOBA_W:], qb_ref[...])
        tail_pg = pg - (PAGES_PER_STEP - ppb)
        if tail_pg >= 0:
            bias = jnp.where(j == nstep - 1, tail_ref[tail_pg * PAGE_SIZE:(tail_pg + 1) * PAGE_SIZE, :], far)
        else:
            bias = far
        s = s + bias
        m_pg = _colmax(s)
        p = jnp.exp(s - m_pg)
        page_m.append(m_pg)
        page_l.append(_fold8(p))
        pr = _mxu_round(p)
        vr = _mxu_round(v_ref[...])
        page_acc.append([_fold8(pr[:, r:r + 1] * vr[:, (r // 2) * DIFF_VDIM:(r // 2 + 1) * DIFF_VDIM])
                         for r in range(N_SUB)])
    m_old = m_ref[...]
    m_new = m_old
    for m_pg in page_m:
        m_new = jnp.maximum(m_new, m_pg)
    alpha = jnp.exp(m_old - m_new)
    weights = [jnp.exp(m_pg - m_new) for m_pg in page_m]
    l_ref[...] = alpha * l_ref[...] + sum(w * l for w, l in zip(weights, page_l))
    m_ref[...] = m_new
    for r in range(N_SUB):
        acc_ref[r] = alpha[:, r:r + 1] * acc_ref[r] + sum(w[:, r:r + 1] * acc[r]
                                                          for w, acc in zip(weights, page_acc))

    @pl.when(j == nstep - 1)
    def _():
        kd = jnp.broadcast_to(knew_ref[:, MOBA_W:], (SUBLANES, DIFF_W))
        s_self = _dot(kd, qb_ref[...])[0:1] + bias0_ref[...]
        m_old = m_ref[...]
        m_new = jnp.maximum(m_old, s_self)
        alpha = jnp.exp(m_old - m_new)
        p_self = jnp.exp(s_self - m_new)
        l = alpha * jnp.sum(l_ref[...], axis=0, keepdims=True) + p_self
        pr_self = _mxu_round(p_self)
        v_self = _mxu_round(vnew_ref[:, MOBA_W:])
        lam = _lambda(lam_ref, lam_init)
        for h in range(DIFF_HEADS):
            vcols = slice(h * DIFF_VDIM, (h + 1) * DIFF_VDIM)
            parts = []
            for c in range(2):
                r = 2 * h + c
                num = (alpha[:, r:r + 1] * jnp.sum(acc_ref[r], axis=0, keepdims=True)
                       + pr_self[:, r:r + 1] * v_self[:, vcols])
                parts.append(num / l[:, r:r + 1])
            oh = parts[0] - lam * parts[1]
            o_ref[:, vcols] = (oh * lax.rsqrt(jnp.mean(oh * oh, -1, keepdims=True) + LN_EPS)
                               * g_ref[...] * (1.0 - lam_init))
        head_row = lax.broadcasted_iota(jnp.int32, (MOBA_HEADS, MOBA_W), 0)
        head_of_lane = lax.broadcasted_iota(jnp.int32, (MOBA_HEADS, MOBA_W), 1) // HEAD_DIM
        seg = jnp.where(head_row == head_of_lane, 1.0, 0.0)
        gate = lax.dot_general(seg, gacc_ref[...], (((1,), (1,)), ((), ())), precision=HIGHEST,
                               preferred_element_type=F32)
        ids = lax.broadcasted_iota(jnp.int32, gate.shape, 1).astype(F32)
        out_lane = lax.broadcasted_iota(jnp.int32, (MOBA_HEADS, LANES), 1)
        picked = jnp.zeros((MOBA_HEADS, LANES), F32)
        for t in range(MOBA_TOPK):
            mx = jnp.max(gate, axis=-1, keepdims=True)
            idx = jnp.min(jnp.where(gate == mx, ids, float(gate.shape[-1])), axis=-1, keepdims=True)
            picked = jnp.where(out_lane == t, idx, picked)
            gate = jnp.where(ids == idx, NEG_INF, gate)
        sel_ref[...] = picked.astype(jnp.int32)


def _sample_main(page_table, q_s, k_s, v_s, cache_k, cache_v, layer, tab_d, lam_vecs, subln_g, lam_init):
    db, n_pages = page_table.shape
    past = n_pages * PAGE_SIZE
    assert n_pages % PAGES_PER_STEP == 0 and past // MOBA_BLOCK >= MOBA_TOPK
    nblk = past // MOBA_BLOCK
    nstep = n_pages // PAGES_PER_STEP
    tab_sub = jnp.repeat(tab_d, 2, axis=0)
    tail = _bias_lookup(tab_sub, MOBA_BLOCK - jnp.arange(MOBA_BLOCK)).T
    tail = jnp.pad(tail, ((0, 0), (0, LANES - N_SUB)))
    bias_self = jnp.pad(tab_sub[:, 0].reshape(1, N_SUB), ((0, 0), (0, LANES - N_SUB)))

    def page(pg, col):
        return lambda b, j, pt: (pt[b * n_pages + PAGES_PER_STEP * j + pg], layer, 0, col)

    tok = lambda b, j, pt: (b, 0, 0)
    const2 = lambda b, j, pt: (0, 0)
    grid_spec = pltpu.PrefetchScalarGridSpec(
        num_scalar_prefetch=1,
        grid=(db, nstep),
        in_specs=([pl.BlockSpec((None, 1, MIX_W), tok)] * 3
                  + [pl.BlockSpec((None, None, PAGE_SIZE, MIX_W), page(pg, 0)) for pg in range(PAGES_PER_STEP)]
                  + [pl.BlockSpec((None, None, PAGE_SIZE, DIFF_W), page(pg, 1)) for pg in range(PAGES_PER_STEP)]
                  + [pl.BlockSpec((MOBA_BLOCK, LANES), const2), pl.BlockSpec((1, LANES), const2),
                     pl.BlockSpec((4, HEAD_DIM), const2), pl.BlockSpec((1, DIFF_VDIM), const2)]),
        out_specs=[pl.BlockSpec((None, 1, DIFF_W), tok),
                   pl.BlockSpec((None, MOBA_HEADS, LANES), tok)],
        scratch_shapes=[pltpu.VMEM((DIFF_W, LANES), BF16), pltpu.VMEM((nblk, MOBA_W), F32),
                        pltpu.VMEM((1, LANES), F32), pltpu.VMEM((SUBLANES, LANES), F32),
                        pltpu.VMEM((N_SUB, SUBLANES, DIFF_VDIM), F32)],
    )
    o_d, sel = pl.pallas_call(
        functools.partial(_sample_main_kernel, lam_init=lam_init),
        grid_spec=grid_spec,
        out_shape=[jax.ShapeDtypeStruct((db, 1, DIFF_W), F32),
                   jax.ShapeDtypeStruct((db, MOBA_HEADS, LANES), jnp.int32)],
        compiler_params=_cparams(("parallel", "arbitrary")),
        name="sample_main",
    )(page_table.reshape(-1), q_s, k_s, v_s, *([cache_k] * PAGES_PER_STEP), *([cache_v] * PAGES_PER_STEP),
      tail, bias_self, lam_vecs, subln_g)
    return o_d.reshape(db, DIFF_W), sel[:, :, :MOBA_TOPK]


def _sample_moba_kernel(pt_ref, sel_ref, q_ref, knew_ref, vnew_ref, *refs):
    del pt_ref, sel_ref
    n_pg = MOBA_TOPK * (MOBA_BLOCK // PAGE_SIZE)
    k_refs = refs[:n_pg]
    v_refs = refs[n_pg:2 * n_pg]
    bias_refs = refs[2 * n_pg:2 * n_pg + MOBA_TOPK]
    bias0_ref, o_ref = refs[2 * n_pg + MOBA_TOPK:]
    h = pl.program_id(1)
    par = h % 2
    lane = lax.broadcasted_iota(jnp.int32, (1, LANES), 1)
    mine = (lane >= par * HEAD_DIM) & (lane < (par + 1) * HEAD_DIM)
    qm = jnp.where(mine, q_ref[...], 0.0) * ATTN_SCALE
    q_rep = jnp.broadcast_to(qm, (LANES, LANES)).T.astype(BF16)
    m = jnp.full((1, LANES), M_INIT, F32)
    l8 = jnp.zeros((SUBLANES, LANES), F32)
    acc = jnp.zeros((SUBLANES, LANES), F32)
    ppb = MOBA_BLOCK // PAGE_SIZE
    for pg in range(n_pg):
        brow = bias_refs[pg // ppb][:, (pg % ppb) * PAGE_SIZE:(pg % ppb + 1) * PAGE_SIZE]
        s = _dot(k_refs[pg][...], q_rep)
        s = s + jnp.broadcast_to(brow, (PAGE_SIZE, PAGE_SIZE)).T
        m_new = jnp.maximum(m, _colmax(s))
        p = jnp.exp(s - m_new)
        alpha = jnp.exp(m - m_new)
        l8 = alpha * l8 + _fold8(p)
        acc = alpha * acc + _fold8(_mxu_round(p) * _mxu_round(v_refs[pg][...]))
        m = m_new
    s_self = jnp.sum(_mxu_round(qm) * _mxu_round(knew_ref[...]), axis=-1, keepdims=True) + bias0_ref[:, 0:1]
    m_new = jnp.maximum(m, s_self)
    alpha = jnp.exp(m - m_new)
    p_self = jnp.exp(s_self - m_new)
    o = ((alpha * jnp.sum(acc, axis=0, keepdims=True) + _mxu_round(p_self) * _mxu_round(vnew_ref[...]))
         / (alpha * jnp.sum(l8, axis=0, keepdims=True) + p_self))
    o_ref[...] = jnp.where(par == 0, o[:, :HEAD_DIM], o[:, HEAD_DIM:])


def _sample_moba(page_table, sel, q_s, k_s, v_s, cache_k, cache_v, layer, tab_m):
    db, n_pages = page_table.shape
    past = n_pages * PAGE_SIZE
    ppb = MOBA_BLOCK // PAGE_SIZE
    n_pg = MOBA_TOPK * ppb
    bias_past = _bias_lookup(tab_m, past - jnp.arange(past)).reshape(MOBA_HEADS, 1, past)
    bias_self = jnp.broadcast_to(tab_m[:, 0:1, None], (MOBA_HEADS, 1, LANES))

    def sel_at(b, h, t, sl):
        return sl[(b * MOBA_HEADS + h) * MOBA_TOPK + t]

    def page(pg):
        return lambda b, h, pt, sl: (pt[b * n_pages + ppb * sel_at(b, h, pg // ppb, sl) + pg % ppb], layer, 0, h // 2)

    def bias_blk(t):
        return lambda b, h, pt, sl: (h, 0, sel_at(b, h, t, sl))

    tok = lambda b, h, pt, sl: (b, 0, h // 2)
    grid_spec = pltpu.PrefetchScalarGridSpec(
        num_scalar_prefetch=2,
        grid=(db, MOBA_HEADS),
        in_specs=([pl.BlockSpec((None, 1, LANES), tok)] * 3
                  + [pl.BlockSpec((None, None, PAGE_SIZE, LANES), page(pg)) for pg in range(n_pg)] * 2
                  + [pl.BlockSpec((None, 1, MOBA_BLOCK), bias_blk(t)) for t in range(MOBA_TOPK)]
                  + [pl.BlockSpec((None, 1, LANES), lambda b, h, pt, sl: (h, 0, 0))]),
        out_specs=pl.BlockSpec((None, None, 1, HEAD_DIM), lambda b, h, pt, sl: (b, h, 0, 0)),
    )
    o_m = pl.pallas_call(
        _sample_moba_kernel,
        grid_spec=grid_spec,
        out_shape=jax.ShapeDtypeStruct((db, MOBA_HEADS, 1, HEAD_DIM), F32),
        compiler_params=_cparams(("parallel", "arbitrary")),
        name="sample_moba",
    )(page_table.reshape(-1), sel.reshape(-1), q_s, k_s, v_s,
      *([cache_k] * n_pg), *([cache_v] * n_pg), *([bias_past] * MOBA_TOPK), bias_self)
    return o_m.reshape(db, MOBA_W)


def _post_attn_kernel(h_ref, mm_ref, md_ref, wo_ref, g_ref, b_ref, wr_ref, br_ref,
                      a_ref, eid_ref, gate_ref, *, alpha, transposed):
    if transposed:
        mm = mm_ref[...].astype(F32).T
        md = md_ref[...].astype(F32).T
    else:
        mm, md = mm_ref[...], md_ref[...]
    mix = _dot(mm, wo_ref[:MOBA_W, :]) + _dot(md, wo_ref[MOBA_W:, :])
    a = _layer_norm(alpha * h_ref[...] + mix, g_ref[...], b_ref[...])
    _store_tiles(a_ref, a)
    logits = _dot(a, wr_ref[...]) + br_ref[...]
    lane = lax.broadcasted_iota(jnp.int32, logits.shape, 1).astype(F32)
    none = float(LANES)
    gl = jnp.where(lane < N_GROUPS, logits, NEG_INF)
    gmax = jnp.max(gl, axis=-1, keepdims=True)
    g_gate = 1.0 / jnp.sum(jnp.exp(gl - gmax), axis=-1, keepdims=True)
    g_idx = jnp.min(jnp.where(gl == gmax, lane, none), axis=-1, keepdims=True)
    lo = N_GROUPS + EXPERTS_PER_GROUP * g_idx
    el = jnp.where(lane >= lo, jnp.where(lane < lo + EXPERTS_PER_GROUP, logits, NEG_INF), NEG_INF)
    m1 = jnp.max(el, axis=-1, keepdims=True)
    i1 = jnp.min(jnp.where(el == m1, lane, none), axis=-1, keepdims=True)
    el2 = jnp.where(lane == i1, NEG_INF, el)
    m2 = jnp.max(el2, axis=-1, keepdims=True)
    i2 = jnp.min(jnp.where(el2 == m2, lane, none), axis=-1, keepdims=True)
    r = jnp.exp(m2 - m1)
    w1 = g_gate / (1.0 + r)
    w2 = g_gate * r / (1.0 + r)
    eid_ref[...] = jnp.where(lane == 0.0, i1 - N_GROUPS, jnp.where(lane == 1.0, i2 - N_GROUPS, 0.0)).astype(jnp.int32)
    gate_ref[...] = jnp.where(lane == 0.0, w1, jnp.where(lane == 1.0, w2, 0.0))


def _post_attn(h, mix_m, mix_d, w_out, ln_g, ln_b, w_router, b_router, alpha, tm, transposed):
    m = h.shape[0]
    row = lambda i: (i, 0)
    const = lambda i: (0, 0)
    if transposed:
        assert tm == T
        mix_specs = [pl.BlockSpec((None, MOBA_W, T), lambda i: (i, 0, 0)),
                     pl.BlockSpec((None, DIFF_W, T), lambda i: (i, 0, 0))]
    else:
        mix_specs = [pl.BlockSpec((tm, MOBA_W), row), pl.BlockSpec((tm, DIFF_W), row)]
    return pl.pallas_call(
        functools.partial(_post_attn_kernel, alpha=alpha, transposed=transposed),
        grid=(m // tm,),
        in_specs=[pl.BlockSpec((tm, D_MODEL), row)] + mix_specs + [
            pl.BlockSpec((MIX_W, D_MODEL), const),
            pl.BlockSpec((1, D_MODEL), const), pl.BlockSpec((1, D_MODEL), const),
            pl.BlockSpec((D_MODEL, LANES), const), pl.BlockSpec((1, LANES), const)],
        out_specs=[pl.BlockSpec((tm * N_CHUNK, LANES), row), pl.BlockSpec((tm, LANES), row),
                   pl.BlockSpec((tm, LANES), row)],
        out_shape=[jax.ShapeDtypeStruct((m * N_CHUNK, LANES), F32), jax.ShapeDtypeStruct((m, LANES), jnp.int32),
                   jax.ShapeDtypeStruct((m, LANES), F32)],
        compiler_params=_cparams(("parallel",)),
        name="post_attn",
    )(h, mix_m, mix_d, w_out, ln_g, ln_b, w_router, b_router)


def _moe_kernel(code_ref, blk_e_ref, blk_src_ref, blk_n_ref, nblk_ref, a_hbm, w1_ref, w3_ref, w2_ref, y_hbm,
                xbuf, ybuf, w1b, w3b, w2b, gsem, ssem, *, rb, n_tok):
    i = pl.program_id(0)
    n_used = nblk_ref[0]

    def tile(t):
        start = t * N_CHUNK
        return pl.ds(start if isinstance(t, int) else pl.multiple_of(start, N_CHUNK), N_CHUNK)

    def gather_row(base, r, slot):
        tok = code_ref[base + r] >> 1
        return pltpu.make_async_copy(a_hbm.at[tile(tok)], xbuf.at[slot, tile(r)], gsem.at[slot])

    def scatter_row(base, r, n_valid, slot):
        code = code_ref[base + r]
        dst = jnp.where(r < n_valid, (code >> 1) + (code & 1) * n_tok, 2 * n_tok + slot * rb + r)
        return pltpu.make_async_copy(ybuf.at[slot, tile(r)], y_hbm.at[tile(dst)], ssem.at[slot])

    def for_rows(fn):
        def body(r, carry):
            fn(r)
            return carry
        lax.fori_loop(0, rb, body, 0, unroll=SUBLANES)

    def wait_gather(slot):
        pltpu.make_async_copy(xbuf.at[slot], xbuf.at[slot], gsem.at[slot]).wait()

    def wait_scatter(slot):
        pltpu.make_async_copy(ybuf.at[slot], ybuf.at[slot], ssem.at[slot]).wait()

    def step(slot):
        other = 1 - slot

        def first_step():
            base0 = blk_src_ref[0]
            for_rows(lambda r: gather_row(base0, r, 0).start())
            ybuf[1] = jnp.zeros(ybuf.shape[1:], F32)
            for half in range(2):
                for_rows(lambda r, half=half: pltpu.make_async_copy(
                    ybuf.at[1, tile(r)], y_hbm.at[tile(2 * n_tok + half * rb + r)], ssem.at[1]).start())
                wait_scatter(1)

        if slot == 0:
            pl.when(i == 0)(first_step)

        e = blk_e_ref[i]

        @pl.when((i == 0) | (e != blk_e_ref[jnp.maximum(i - 1, 0)]))
        def _():
            w1b[...] = w1_ref[...].astype(BF16)
            w3b[...] = w3_ref[...].astype(BF16)
            w2b[...] = w2_ref[...].astype(BF16)

        wait_gather(slot)

        @pl.when(i >= 1)
        def _():
            wait_scatter(slot)

        nxt_base = blk_src_ref[jnp.minimum(i + 1, pl.num_programs(0) - 1)]
        prv = jnp.maximum(i - 1, 0)
        prv_base = blk_src_ref[prv]
        prv_valid = jnp.where(i >= 1, blk_n_ref[prv], 0)

        for_rows(lambda r: gather_row(nxt_base, r, other).start())
        x = _load_tiles(xbuf.at[slot]).astype(BF16)
        h1 = jnp.dot(x, w1b[...], preferred_element_type=F32)
        h3 = jnp.dot(x, w3b[...], preferred_element_type=F32)
        hidden = (h1 * jax.nn.sigmoid(h1) * h3).astype(BF16)
        for r in range(rb):
            scatter_row(prv_base, r, prv_valid, other).start(priority=1)
        y = jnp.dot(hidden, w2b[...], preferred_element_type=F32)
        _store_tiles(ybuf.at[slot], y)

        @pl.when(i == n_used - 1)
        def _():
            base = blk_src_ref[i]
            n_valid = blk_n_ref[i]
            for_rows(lambda r: scatter_row(base, r, n_valid, slot).start())
            wait_scatter(slot)
            wait_scatter(other)
            wait_gather(other)

    for parity in range(2):
        pl.when((i < n_used) & (i % 2 == parity))(functools.partial(step, parity))


def _route_tables(eid, rb):
    n = eid.shape[0]
    m = n * 2
    e_flat = eid.reshape(m)
    keys = lax.sort(e_flat * m + jnp.arange(m, dtype=jnp.int32))
    code = keys % m
    experts = jnp.arange(N_EXPERTS, dtype=jnp.int32)
    counts = jnp.sum((e_flat[:, None] == experts[None, :]).astype(jnp.int32), axis=0)
    padded = (counts + rb - 1) // rb * rb
    start = jnp.cumsum(counts) - counts
    p_end = jnp.cumsum(padded)
    p_start = p_end - padded
    n_blk = -(-(m + N_EXPERTS * (rb - 1)) // rb)
    blk_row = jnp.arange(n_blk, dtype=jnp.int32) * rb
    blk_e = jnp.minimum(jnp.sum((p_end[None, :] <= blk_row[:, None]).astype(jnp.int32), axis=1), N_EXPERTS - 1)
    onehot = (blk_e[:, None] == experts[None, :]).astype(jnp.int32)
    pick = lambda table: jnp.sum(onehot * table[None, :], axis=1)
    off = blk_row - pick(p_start)
    blk_src = jnp.clip(pick(start) + off, 0, m - 1).astype(jnp.int32)
    blk_n = jnp.clip(pick(counts) - off, 0, rb).astype(jnp.int32)
    n_used = (p_end[-1:] // rb).astype(jnp.int32)
    code = jnp.pad(code.astype(jnp.int32), (0, rb))
    return code, blk_e.astype(jnp.int32), blk_src, blk_n, n_used, n_blk


def _moe(a_tiles, eid, w1, w3, w2, rb):
    n = a_tiles.shape[0] // N_CHUNK
    code, blk_e, blk_src, blk_n, n_used, n_blk = _route_tables(eid, rb)
    wmap = lambda i, code, be, bs, bn, nu: (be[i], 0, 0)
    grid_spec = pltpu.PrefetchScalarGridSpec(
        num_scalar_prefetch=5,
        grid=(n_blk,),
        in_specs=[pl.BlockSpec(memory_space=pl.ANY),
                  pl.BlockSpec((None, D_MODEL, D_EXPERT), wmap),
                  pl.BlockSpec((None, D_MODEL, D_EXPERT), wmap),
                  pl.BlockSpec((None, D_EXPERT, D_MODEL), wmap)],
        out_specs=pl.BlockSpec(memory_space=pl.ANY),
        scratch_shapes=[pltpu.VMEM((2, rb * N_CHUNK, LANES), F32), pltpu.VMEM((2, rb * N_CHUNK, LANES), F32),
                        pltpu.VMEM((D_MODEL, D_EXPERT), BF16), pltpu.VMEM((D_MODEL, D_EXPERT), BF16),
                        pltpu.VMEM((D_EXPERT, D_MODEL), BF16),
                        pltpu.SemaphoreType.DMA((2,)), pltpu.SemaphoreType.DMA((2,))],
    )
    return pl.pallas_call(
        functools.partial(_moe_kernel, rb=rb, n_tok=n),
        grid_spec=grid_spec,
        out_shape=jax.ShapeDtypeStruct(((2 * n + 2 * rb) * N_CHUNK, LANES), F32),
        compiler_params=_cparams(("arbitrary",)),
        name="moe",
    )(code, blk_e, blk_src, blk_n, n_used, a_tiles, w1, w3, w2)


def _finish_kernel(a_ref, y0_ref, y1_ref, gate_ref, p_ref, g_ref, b_ref, wpg_ref, bpg_ref, wple_ref, o_ref,
                   *, alpha):
    gates = gate_ref[...]
    moe = _load_tiles(y0_ref) * gates[:, 0:1] + _load_tiles(y1_ref) * gates[:, 1:2]
    c = _layer_norm(alpha * _load_tiles(a_ref) + moe, g_ref[...], b_ref[...])
    gate = jax.nn.sigmoid(_dot(c, wpg_ref[...]) + bpg_ref[...])
    o_ref[...] = c + gate * _dot(p_ref[...], wple_ref[...])


def _finish(a, y, gates, p, ln_g, ln_b, w_pg, b_pg, w_ple, alpha, tm):
    m = a.shape[0] // N_CHUNK
    assert m % tm == 0
    row = lambda i: (i, 0)
    const = lambda i: (0, 0)
    tiles = (tm * N_CHUNK, LANES)
    return pl.pallas_call(
        functools.partial(_finish_kernel, alpha=alpha),
        grid=(m // tm,),
        in_specs=[pl.BlockSpec(tiles, row), pl.BlockSpec(tiles, row),
                  pl.BlockSpec(tiles, lambda i: (m // tm + i, 0)),
                  pl.BlockSpec((tm, LANES), row), pl.BlockSpec((tm, D_PLE), row),
                  pl.BlockSpec((1, D_MODEL), const), pl.BlockSpec((1, D_MODEL), const),
                  pl.BlockSpec((D_MODEL, D_MODEL), const), pl.BlockSpec((1, D_MODEL), const),
                  pl.BlockSpec((D_PLE, D_MODEL), const)],
        out_specs=pl.BlockSpec((tm, D_MODEL), row),
        out_shape=jax.ShapeDtypeStruct((m, D_MODEL), F32),
        compiler_params=_cparams(("parallel",)),
        name="finish",
    )(a, y, y, gates, p, ln_g, ln_b, w_pg, b_pg, w_ple)


def _pack_router(w_group, b_group, w_er, b_er):
    w = jnp.concatenate([w_group, jnp.transpose(w_er, (1, 0, 2)).reshape(D_MODEL, N_EXPERTS)], axis=1)
    b = jnp.concatenate([b_group, b_er.reshape(N_EXPERTS)])
    pad = LANES - w.shape[1]
    return jnp.pad(w, ((0, 0), (0, pad))), jnp.pad(b, (0, pad)).reshape(1, LANES)


def kernel(x_prompt, x_sample, cache_k, cache_v, page_table, p_prompt, p_sample, rel_bias, w_in, w_out, lam_q1, lam_k1, lam_q2, lam_k2, subln_g, ln1_g, ln1_b, w_group, b_group, w_erouter, b_erouter, w1, w3, w2, ln2_g, ln2_b, w_ple, w_pg, b_pg):
    depth = w_in.shape[0]
    b, s, _ = x_prompt.shape
    db, ds, _ = x_sample.shape
    assert ds == 1 and s % T == 0
    alpha = (2 * depth) ** 0.25
    tab_m = rel_bias[:, :MOBA_HEADS].T
    tab_d = rel_bias[:, MOBA_HEADS:].T
    hp = x_prompt.reshape(b * s, D_MODEL)
    hs = x_sample.reshape(db, D_MODEL)
    kp_rows, vp_rows, ks_rows, vs_rows = [], [], [], []
    for i in range(depth):
        lam_init = 0.8 - 0.6 * math.exp(-0.3 * i)
        lam_vecs = jnp.stack([lam_q1[i], lam_k1[i], lam_q2[i], lam_k2[i]]).astype(F32)
        g_sub = subln_g[i].reshape(1, DIFF_VDIM)
        w_router, b_router = _pack_router(w_group[i], b_group[i], w_erouter[i], b_erouter[i])
        ln1 = (ln1_g[i].reshape(1, D_MODEL), ln1_b[i].reshape(1, D_MODEL))
        ln2 = (ln2_g[i].reshape(1, D_MODEL), ln2_b[i].reshape(1, D_MODEL))
        b_pg_i = b_pg[i].reshape(1, D_MODEL)

        w_in_b, w_out_b = w_in[i].astype(BF16), w_out[i].astype(BF16)
        w_pg_b, w_ple_b, w_router_b = w_pg[i].astype(BF16), w_ple[i].astype(BF16), w_router.astype(BF16)

        k_p, v_p, qt, kb, vt, kmean = _qkv_prompt(hp, w_in_b)
        mix_m = _moba_prompt(qt, kb, vt, kmean, tab_m, b, s)
        mix_d = _diff_prompt(qt, kb, vt, tab_d, lam_vecs, g_sub, lam_init, b, s)
        a_p, eid_p, gate_p = _post_attn(hp, mix_m, mix_d, w_out_b, *ln1, w_router_b, b_router, alpha, T, True)
        y_p = _moe(a_p, eid_p[:, :2], w1[i], w3[i], w2[i], 512)
        hp = _finish(a_p, y_p, gate_p, p_prompt[i].reshape(b * s, D_PLE), *ln2, w_pg_b, b_pg_i, w_ple_b,
                     alpha, 512)

        qkv_s = _mm_small(hs, w_in_b, MIX_W)
        q_s = qkv_s[:, :MIX_W].reshape(db, 1, MIX_W)
        k_s = qkv_s[:, MIX_W:2 * MIX_W].reshape(db, 1, MIX_W)
        v_s = qkv_s[:, 2 * MIX_W:].reshape(db, 1, MIX_W)
        o_d, sel = _sample_main(page_table, q_s, k_s, v_s, cache_k, cache_v, i, tab_d, lam_vecs, g_sub, lam_init)
        o_m = _sample_moba(page_table, sel, q_s, k_s, v_s, cache_k, cache_v, i, tab_m)
        a_s, eid_s, gate_s = _post_attn(hs, o_m, o_d, w_out_b, *ln1, w_router_b, b_router, alpha, db, False)
        y_s = _moe(a_s, eid_s[:, :2], w1[i], w3[i], w2[i], 64)
        hs = _finish(a_s, y_s, gate_s, p_sample[i].reshape(db, D_PLE), *ln2, w_pg_b, b_pg_i, w_ple_b, alpha, db)

        kp_rows.append(k_p.reshape(b, s, MIX_W))
        vp_rows.append(v_p.reshape(b, s, MIX_W))
        ks_rows.append(k_s.reshape(db, ds, MIX_W))
        vs_rows.append(v_s.reshape(db, ds, MIX_W))
    return (hp.reshape(b, s, D_MODEL), hs.reshape(db, ds, D_MODEL),
            jnp.stack(kp_rows, axis=1), jnp.stack(vp_rows, axis=1),
            jnp.stack(ks_rows, axis=1), jnp.stack(vs_rows, axis=1))
```
